```python
import math
import jax, jax.numpy as jnp
from jax import lax
import numpy as np


D_MODEL = 1024
BATCH = 16
SEQ = 2048
DEPTH = 1
DEC_BATCH = 32
DEC_SEQ = 8
PAST_LEN = 16384
PAGE_SIZE = 128

MIX_WIDTH = D_MODEL
SB_HEAD_DIM = 64
SB_WIDTH = MIX_WIDTH // 2
SB_HEADS = SB_WIDTH // SB_HEAD_DIM
SB_SCALE = SB_HEAD_DIM ** -0.5
SB_BIAS_INIT = -7.0
CONV_CH = MIX_WIDTH - SB_WIDTH
CONV_GROUP_DIM = 64
CONV_GROUPS = CONV_CH // CONV_GROUP_DIM
CONV_K = 3
IN_WIDTH = 3 * SB_WIDTH + 3 * CONV_CH
N_MEM = 256
CA_HEADS = 4
CA_HEAD_DIM = 128
CA_WIDTH = CA_HEADS * CA_HEAD_DIM
CA_SCALE = CA_HEAD_DIM ** -0.5
D_FF = ((8 * D_MODEL // 3 + 255) // 256) * 256
BLOCK_Q = 128
EPS = 1e-6

kernel_name = "hymba_stickbreak_shortconv_decode_step"


def rmsnorm(x, g):
    xf = x.astype(jnp.float32)
    y = xf * lax.rsqrt(jnp.mean(xf * xf, axis=-1, keepdims=True) + EPS)
    return (y * g.astype(jnp.float32)).astype(x.dtype)


def sb_logits(q, k, sb_bias):
    z = jnp.einsum("bqhd,bkhd->bhqk", q, k).astype(jnp.float32) * SB_SCALE
    return z + sb_bias.astype(jnp.float32)[None, :, None, None]


def sb_block(z, mask, v, carry):
    log_om = jnp.where(mask, jax.nn.log_sigmoid(-z), 0.0)
    suffix = lax.cumsum(log_om, axis=3, reverse=True) - log_om + carry[..., None]
    a = jnp.where(mask, jnp.exp(jax.nn.log_sigmoid(z) + suffix), 0.0)
    o = jnp.einsum("bhqk,bkhd->bqhd", a, v.astype(jnp.float32))
    return o, carry + jnp.sum(log_om, axis=3)


def sb_attention_prompt(q, k, v, sb_bias):
    b, s, h, d = q.shape
    nb = s // BLOCK_Q
    q_blocks = q.reshape(b, nb, BLOCK_Q, h, d).swapaxes(0, 1)
    key_pos = jnp.arange(s)

    def one_block(args):
        q_blk, blk = args
        query_pos = blk * BLOCK_Q + jnp.arange(BLOCK_Q)
        mask = key_pos[None, :] < query_pos[:, None]
        z = sb_logits(q_blk, k, sb_bias)
        o, _ = sb_block(z, mask, v, jnp.zeros(z.shape[:3], jnp.float32))
        return o

    o = lax.map(one_block, (q_blocks, jnp.arange(nb)))
    return o.swapaxes(0, 1).reshape(b, s, h * d).astype(q.dtype)


def sb_attention_sample(q, k_new, v_new, cache_k, cache_v, page_table, sb_bias):
    b, t, h, d = q.shape
    new_pos = jnp.arange(t)
    mask = new_pos[None, :] < new_pos[:, None]
    z = sb_logits(q, k_new, sb_bias)
    o, carry = sb_block(z, mask, v_new, jnp.zeros(z.shape[:3], jnp.float32))

    def page_step(state, phys):
        o_acc, carry = state
        kp = cache_k[phys]
        vp = cache_v[phys]
        zp = sb_logits(q, kp, sb_bias)
        o_p, carry = sb_block(zp, True, vp, carry)
        return (o_acc + o_p, carry), None

    (o, _), _ = lax.scan(page_step, (o, carry), page_table.T, reverse=True)
    return o.reshape(b, t, h * d).astype(q.dtype)


def mixer_inputs(x, g_mix, w_in):
    b, s, _ = x.shape
    hn = rmsnorm(x, g_mix)
    p = hn @ w_in
    q, k, v, bg, cg, xv = jnp.split(
        p, [SB_WIDTH, 2 * SB_WIDTH, 3 * SB_WIDTH, 3 * SB_WIDTH + CONV_CH, 3 * SB_WIDTH + 2 * CONV_CH], axis=-1)
    heads = lambda t_: t_.reshape(b, s, SB_HEADS, SB_HEAD_DIM)
    return heads(q), heads(k), heads(v), bg, cg * xv


def causal_conv3(u_ext, w_conv):
    s = u_ext.shape[1] - (CONV_K - 1)
    return w_conv[0] * u_ext[:, :s] + w_conv[1] * u_ext[:, 1:s + 1] + w_conv[2] * u_ext[:, 2:]


def mixer_output(x, attn, bg, conv_y, g_attn_out, g_conv_out, w_out):
    a = rmsnorm(attn, g_attn_out)
    c = rmsnorm(bg * conv_y, g_conv_out)
    return x + jnp.concatenate([a, c], axis=-1) @ w_out


def memory_kv(mem, g_mem, w_ck, w_cv):
    b = mem.shape[0]
    mn = rmsnorm(mem, g_mem)
    mk = (mn @ w_ck).reshape(b, N_MEM, CA_HEADS, CA_HEAD_DIM)
    mv = (mn @ w_cv).reshape(b, N_MEM, CA_HEADS, CA_HEAD_DIM)
    return mk, mv


def cross_attention(x, mk, mv, g_ca, w_cq, w_co):
    b, s, _ = x.shape
    q = (rmsnorm(x, g_ca) @ w_cq).reshape(b, s, CA_HEADS, CA_HEAD_DIM)
    sc = jnp.einsum("bqhd,bmhd->bhqm", q, mk).astype(jnp.float32) * CA_SCALE
    p = jax.nn.softmax(sc, axis=-1)
    o = jnp.einsum("bhqm,bmhd->bqhd", p, mv.astype(jnp.float32)).astype(x.dtype)
    return x + o.reshape(b, s, CA_WIDTH) @ w_co


def swiglu_ffn(x, g_ffn, w_gate, w_up, w_down):
    hn = rmsnorm(x, g_ffn)
    return x + (jax.nn.silu(hn @ w_gate) * (hn @ w_up)) @ w_down


def setup_inputs(seed: int = 0) -> dict:
    key = jax.random.key(seed)
    ks = jax.random.split(key, 32)
    n_pages = PAST_LEN // PAGE_SIZE
    n_used = DEC_BATCH * n_pages
    n_pool = n_used + (n_used + 3) // 4

    def normal(k, shape, scale=1.0):
        return scale * jax.random.normal(k, shape, jnp.float32)

    def gain(k, dim):
        return 1.0 + 0.02 * jax.random.normal(k, (DEPTH, dim), jnp.float32)

    page_table = jax.random.permutation(ks[8], n_pool)[:n_used].reshape(DEC_BATCH, n_pages).astype(jnp.int32)
    return {
        "x_prompt": normal(ks[0], (BATCH, SEQ, D_MODEL)),
        "x_sample": normal(ks[1], (DEC_BATCH, DEC_SEQ, D_MODEL)),
        "mem_prompt": normal(ks[2], (BATCH, N_MEM, D_MODEL)),
        "cache_sb_k": normal(ks[3], (DEPTH, n_pool, PAGE_SIZE, SB_HEADS, SB_HEAD_DIM)),
        "cache_sb_v": normal(ks[4], (DEPTH, n_pool, PAGE_SIZE, SB_HEADS, SB_HEAD_DIM)),
        "cache_mem_k": normal(ks[5], (DEPTH, DEC_BATCH, N_MEM, CA_HEADS, CA_HEAD_DIM)),
        "cache_mem_v": normal(ks[6], (DEPTH, DEC_BATCH, N_MEM, CA_HEADS, CA_HEAD_DIM)),
        "state_conv": normal(ks[7], (DEPTH, DEC_BATCH, CONV_K - 1, CONV_CH)),
        "page_table": page_table,
        "g_mix": gain(ks[10], D_MODEL),
        "w_in": normal(ks[11], (DEPTH, D_MODEL, IN_WIDTH), D_MODEL ** -0.5),
        "sb_bias": SB_BIAS_INIT + 0.1 * jax.random.normal(ks[27], (DEPTH, SB_HEADS), jnp.float32),
        "w_conv": normal(ks[12], (DEPTH, CONV_K, CONV_CH), CONV_K ** -0.5),
        "g_attn_out": gain(ks[13], SB_WIDTH),
        "g_conv_out": gain(ks[14], CONV_CH),
        "w_out": normal(ks[15], (DEPTH, MIX_WIDTH, D_MODEL), MIX_WIDTH ** -0.5),
        "g_mem": gain(ks[16], D_MODEL),
        "g_ca": gain(ks[17], D_MODEL),
        "w_cq": normal(ks[18], (DEPTH, D_MODEL, CA_WIDTH), D_MODEL ** -0.5),
        "w_ck": normal(ks[19], (DEPTH, D_MODEL, CA_WIDTH), D_MODEL ** -0.5),
        "w_cv": normal(ks[20], (DEPTH, D_MODEL, CA_WIDTH), D_MODEL ** -0.5),
        "w_co": normal(ks[21], (DEPTH, CA_WIDTH, D_MODEL), CA_WIDTH ** -0.5),
        "g_ffn": gain(ks[22], D_MODEL),
        "w_gate": normal(ks[23], (DEPTH, D_MODEL, D_FF), D_MODEL ** -0.5),
        "w_up": normal(ks[24], (DEPTH, D_MODEL, D_FF), D_MODEL ** -0.5),
        "w_down": normal(ks[25], (DEPTH, D_FF, D_MODEL), D_FF ** -0.5),
        "g_final": 1.0 + 0.02 * jax.random.normal(ks[26], (D_MODEL,), jnp.float32),
    }


def reference(x_prompt, x_sample, mem_prompt, cache_sb_k, cache_sb_v, cache_mem_k, cache_mem_v,
              state_conv, page_table, g_mix, w_in, sb_bias, w_conv, g_attn_out, g_conv_out, w_out,
              g_mem, g_ca, w_cq, w_ck, w_cv, w_co, g_ffn, w_gate, w_up, w_down, g_final):
    yp, ys = x_prompt, x_sample
    kp_l, vp_l, cp_l, mkp_l, mvp_l, ks_l, vs_l, cs_l = [], [], [], [], [], [], [], []
    for l in range(DEPTH):
        q, k, v, bg, u = mixer_inputs(yp, g_mix[l], w_in[l])
        attn = sb_attention_prompt(q, k, v, sb_bias[l])
        u_ext = jnp.pad(u, ((0, 0), (CONV_K - 1, 0), (0, 0)))
        yp = mixer_output(yp, attn, bg, causal_conv3(u_ext, w_conv[l]), g_attn_out[l], g_conv_out[l], w_out[l])
        mk, mv = memory_kv(mem_prompt, g_mem[l], w_ck[l], w_cv[l])
        yp = cross_attention(yp, mk, mv, g_ca[l], w_cq[l], w_co[l])
        yp = swiglu_ffn(yp, g_ffn[l], w_gate[l], w_up[l], w_down[l])
        kp_l.append(k)
        vp_l.append(v)
        cp_l.append(u_ext[:, -(CONV_K - 1):])
        mkp_l.append(mk)
        mvp_l.append(mv)
        q, k, v, bg, u = mixer_inputs(ys, g_mix[l], w_in[l])
        attn = sb_attention_sample(q, k, v, cache_sb_k[l], cache_sb_v[l], page_table, sb_bias[l])
        u_ext = jnp.concatenate([state_conv[l].astype(u.dtype), u], axis=1)
        ys = mixer_output(ys, attn, bg, causal_conv3(u_ext, w_conv[l]), g_attn_out[l], g_conv_out[l], w_out[l])
        ys = cross_attention(ys, cache_mem_k[l], cache_mem_v[l], g_ca[l], w_cq[l], w_co[l])
        ys = swiglu_ffn(ys, g_ffn[l], w_gate[l], w_up[l], w_down[l])
        ks_l.append(k)
        vs_l.append(v)
        cs_l.append(u_ext[:, -(CONV_K - 1):])
    y_prompt = rmsnorm(yp, g_final)
    y_sample = rmsnorm(ys, g_final)
    return (y_prompt, y_sample, jnp.stack(kp_l), jnp.stack(vp_l), jnp.stack(cp_l), jnp.stack(mkp_l),
            jnp.stack(mvp_l), jnp.stack(ks_l), jnp.stack(vs_l), jnp.stack(cs_l))
```

```python
import functools

import jax
import jax.numpy as jnp
from jax import lax
from jax.experimental import pallas as pl
from jax.experimental.pallas import tpu as pltpu

F32 = jnp.float32
BF16 = jnp.bfloat16

D_MODEL = 1024
SB_HEAD_DIM = 64
SB_WIDTH = 512
SB_HEADS = SB_WIDTH // SB_HEAD_DIM
SB_SCALE = SB_HEAD_DIM ** -0.5
CONV_CH = 512
CONV_K = 3
N_MEM = 256
CA_HEADS = 4
CA_HEAD_DIM = 128
CA_WIDTH = CA_HEADS * CA_HEAD_DIM
CA_SCALE = CA_HEAD_DIM ** -0.5
PAGE_SIZE = 128
EPS = 1e-6

LANES = 128
SUBLANES = 8
MXU_DIM = 256
VMEM_LIMIT_BYTES = 56 * 1024 * 1024

ROW_TILE = 512
SB_BLOCK = 256
FFN_CHUNK = 256
PAGES_PER_STEP = 8


def _cparams(*sem):
    return pltpu.CompilerParams(dimension_semantics=sem, vmem_limit_bytes=VMEM_LIMIT_BYTES)


def _resident(shape):
    return pl.BlockSpec(shape, lambda *_: (0,) * len(shape), pipeline_mode=pl.Buffered(1))


def _rms(x, g):
    ms = jnp.mean(x * x, axis=-1, keepdims=True)
    return x * lax.rsqrt(ms + EPS) * g


def _dot(a, b):
    return jnp.dot(a, b, preferred_element_type=F32)


def _dot_nt(a, b):
    return lax.dot_general(a, b, (((1,), (1,)), ((), ())), preferred_element_type=F32)


def _log_sigmoids(z):
    soft = jnp.log1p(jnp.exp(-jnp.abs(z)))
    return jnp.minimum(z, 0.0) - soft, jnp.minimum(-z, 0.0) - soft


def _suffix_sums(log_om, tmat):
    hi = log_om.astype(BF16)
    lo = (log_om - hi.astype(F32)).astype(BF16)
    return _dot(hi, tmat) + _dot(lo, tmat)


def _strict_lower_ones(n):
    row = lax.broadcasted_iota(jnp.int32, (n, n), 0)
    col = lax.broadcasted_iota(jnp.int32, (n, n), 1)
    return (row > col).astype(BF16)


def _in_proj_body(x_ref, g_ref, w_ref, q_ref, k_ref, v_ref, kb_ref, vb_ref, bg_ref, u_ref):
    hn = _rms(x_ref[...], g_ref[...]).astype(BF16)

    def proj(c):
        return _dot(hn, w_ref[:, c * SB_WIDTH:(c + 1) * SB_WIDTH])

    q_ref[...] = proj(0).astype(BF16)
    k = proj(1)
    k_ref[...] = k
    kb_ref[...] = k.astype(BF16)
    v = proj(2)
    v_ref[...] = v
    vb_ref[...] = v.astype(BF16)
    bg_ref[...] = proj(3)
    u_ref[...] = proj(4) * proj(5)


def _in_proj(x, g, w):
    n = x.shape[0]
    tm = min(ROW_TILE, n)
    row = lambda width: pl.BlockSpec((tm, width), lambda i: (i, 0))
    out = lambda dt: jax.ShapeDtypeStruct((n, SB_WIDTH), dt)
    return pl.pallas_call(
        _in_proj_body,
        grid=(n // tm,),
        in_specs=[row(D_MODEL), _resident((1, D_MODEL)), _resident(w.shape)],
        out_specs=[row(SB_WIDTH)] * 7,
        out_shape=[out(BF16), out(F32), out(F32), out(BF16), out(BF16), out(F32), out(F32)],
        compiler_params=_cparams("parallel"),
        name="in_proj",
    )(x, g, w)


def _sb_prompt_body(bias_ref, q_ref, k_ref, v_ref, o_ref):
    blk = SB_BLOCK
    i = pl.program_id(1)
    row = lax.broadcasted_iota(jnp.int32, (blk, blk), 0)
    col = lax.broadcasted_iota(jnp.int32, (blk, blk), 1)
    readable = col < row
    tmat = _strict_lower_ones(blk)
    left = lax.broadcasted_iota(jnp.int32, (blk, LANES), 1) < SB_HEAD_DIM

    for pair in range(SB_HEADS // 2):
        lanes = pl.ds(pair * LANES, LANES)
        q2 = q_ref[:, lanes]
        outs = []
        for half in range(2):
            head = 2 * pair + half
            qm = jnp.where(left if half == 0 else jnp.logical_not(left), q2, jnp.zeros_like(q2))
            bias = bias_ref[head]

            def block(j, carry, o, diagonal):
                rows = pl.ds(pl.multiple_of(j * blk, blk), blk)
                z = _dot_nt(qm, k_ref[rows, lanes]) * SB_SCALE + bias
                log_beta, log_om = _log_sigmoids(z)
                if diagonal:
                    log_om = jnp.where(readable, log_om, 0.0)
                a = jnp.exp(log_beta + _suffix_sums(log_om, tmat) + carry)
                if diagonal:
                    a = jnp.where(readable, a, 0.0)
                o = o + _dot(a.astype(BF16), v_ref[rows, lanes])
                return carry + jnp.sum(log_om, axis=1, keepdims=True), o

            state = block(i, jnp.zeros((blk, 1), F32), jnp.zeros((blk, LANES), F32), True)
            _, o = lax.fori_loop(
                0, i, lambda jj, st: block(i - 1 - jj, st[0], st[1], False), state)
            outs.append(o)
        o_ref[:, lanes] = jnp.where(left, outs[0], outs[1])


def _sb_prompt(q, kb, vb, bias, batch, seq):
    nq = seq // SB_BLOCK
    return pl.pallas_call(
        _sb_prompt_body,
        grid=(batch, nq),
        in_specs=[
            pl.BlockSpec(memory_space=pltpu.SMEM),
            pl.BlockSpec((SB_BLOCK, SB_WIDTH), lambda b, i: (b * nq + i, 0)),
            pl.BlockSpec((seq, SB_WIDTH), lambda b, i: (b, 0)),
            pl.BlockSpec((seq, SB_WIDTH), lambda b, i: (b, 0)),
        ],
        out_specs=pl.BlockSpec((SB_BLOCK, SB_WIDTH), lambda b, i: (b * nq + i, 0)),
        out_shape=jax.ShapeDtypeStruct((batch * seq, SB_WIDTH), F32),
        compiler_params=_cparams("parallel", "arbitrary"),
        name="sb_prompt",
    )(bias, q, kb, vb)


def _sb_sample_body(pt_ref, q_ref, kn_ref, vn_ref, brow_ref, *refs, t_new):
    del pt_ref
    pps = PAGES_PER_STEP
    k_refs, v_refs = refs[:pps], refs[pps:2 * pps]
    o_ref, qbd_ref, carry_ref, acc_ref = refs[2 * pps:]
    rows = SB_HEADS * t_new
    j = pl.program_id(1)
    tmat = _strict_lower_ones(PAGE_SIZE)
    row_head = lax.broadcasted_iota(jnp.int32, (rows, SB_WIDTH), 0) // t_new
    col_head = lax.broadcasted_iota(jnp.int32, (rows, SB_WIDTH), 1) // SB_HEAD_DIM

    def page(kp, vp, readable):
        z = _dot_nt(qbd_ref[...], kp.astype(BF16)) * SB_SCALE + brow_ref[...]
        log_beta, log_om = _log_sigmoids(z)
        if readable is not None:
            log_om = jnp.where(readable, log_om, 0.0)
        a = jnp.exp(log_beta + _suffix_sums(log_om, tmat) + carry_ref[...])
        if readable is not None:
            a = jnp.where(readable, a, 0.0)
        acc_ref[...] += _dot(a.astype(BF16), vp.astype(BF16))
        carry_ref[...] += jnp.sum(log_om, axis=1, keepdims=True)

    @pl.when(j == 0)
    def _():
        q_rows = jnp.concatenate([q_ref[...]] * SB_HEADS, axis=0)
        qbd_ref[...] = jnp.where(row_head == col_head, q_rows, 0.0).astype(BF16)
        carry_ref[...] = jnp.zeros_like(carry_ref)
        acc_ref[...] = jnp.zeros_like(acc_ref)
        pad = jnp.zeros((PAGE_SIZE - t_new, SB_WIDTH), F32)
        key = lax.broadcasted_iota(jnp.int32, (rows, PAGE_SIZE), 1)
        t = lax.broadcasted_iota(jnp.int32, (rows, PAGE_SIZE), 0) % t_new
        page(jnp.concatenate([kn_ref[...], pad], axis=0),
             jnp.concatenate([vn_ref[...], pad], axis=0), key < t)

    for s in range(pps):
        page(k_refs[s][...], v_refs[s][...], None)

    @pl.when(j == pl.num_programs(1) - 1)
    def _():
        acc = acc_ref[...]
        o = jnp.zeros((t_new, SB_WIDTH), F32)
        for head in range(SB_HEADS):
            band = acc[head * t_new:(head + 1) * t_new]
            o = o + jnp.where(col_head[:t_new] == head, band, 0.0)
        o_ref[...] = o


def _sb_sample(q, k_new, v_new, cache_k, cache_v, page_table, bias, batch, t_new):
    n_pages = page_table.shape[1]
    pps = PAGES_PER_STEP
    rows = SB_HEADS * t_new
    brow = jnp.broadcast_to(jnp.repeat(bias, t_new)[:, None], (rows, PAGE_SIZE))
    new_spec = pl.BlockSpec((t_new, SB_WIDTH), lambda b, j, pt: (b, 0))

    def page_spec(s):
        def index(b, j, pt):
            logical = n_pages - 1 - (j * pps + s)
            return (pt[b * n_pages + logical], 0, 0)
        return pl.BlockSpec((None, PAGE_SIZE, SB_WIDTH), index)

    grid_spec = pltpu.PrefetchScalarGridSpec(
        num_scalar_prefetch=1,
        grid=(batch, n_pages // pps),
        in_specs=[new_spec, new_spec, new_spec,
                  pl.BlockSpec((rows, PAGE_SIZE), lambda b, j, pt: (0, 0))]
                 + [page_spec(s) for s in range(pps)] * 2,
        out_specs=new_spec,
        scratch_shapes=[pltpu.VMEM((rows, SB_WIDTH), BF16),
                        pltpu.VMEM((rows, PAGE_SIZE), F32),
                        pltpu.VMEM((rows, SB_WIDTH), F32)],
    )
    return pl.pallas_call(
        functools.partial(_sb_sample_body, t_new=t_new),
        grid_spec=grid_spec,
        out_shape=jax.ShapeDtypeStruct((batch * t_new, SB_WIDTH), F32),
        compiler_params=_cparams("parallel", "arbitrary"),
        name="sb_sample",
    )(page_table.reshape(-1), q, k_new, v_new, brow,
      *([cache_k] * pps), *([cache_v] * pps))


def _mixer_out_body(x_ref, a_ref, bg_ref, u_ref, uprev_ref, init_ref, wc_ref, ga_ref, gc_ref,
                    wo_ref, h_ref, ext_ref, *, tiles_per_seq):
    tm = u_ref.shape[0]
    i = pl.program_id(0)
    if tiles_per_seq == 1:
        halo = init_ref[0]
    else:
        halo = jnp.where(lax.rem(i, tiles_per_seq) == 0, init_ref[0], uprev_ref[...])
    u = u_ref[...]
    ext_ref[0:SUBLANES, :] = halo
    ext_ref[SUBLANES:SUBLANES + tm, :] = u
    u1 = ext_ref[SUBLANES - 1:SUBLANES - 1 + tm, :]
    u2 = ext_ref[SUBLANES - 2:SUBLANES - 2 + tm, :]
    wc = wc_ref[...]
    conv = wc[0:1] * u2 + wc[1:2] * u1 + wc[2:3] * u
    c = _rms(bg_ref[...] * conv, gc_ref[...]).astype(BF16)
    a = _rms(a_ref[...], ga_ref[...]).astype(BF16)
    h_ref[...] = (x_ref[...] + _dot(a, wo_ref[0:SB_WIDTH, :])
                  + _dot(c, wo_ref[SB_WIDTH:SB_WIDTH + CONV_CH, :]))


def _mixer_out(x, attn, bg, u, init, w_conv, g_attn, g_conv, w_out, seq):
    n = x.shape[0]
    tm = min(ROW_TILE, seq)
    tiles_per_seq = seq // tm
    per8 = tm // SUBLANES
    row = lambda width: pl.BlockSpec((tm, width), lambda i: (i, 0))
    return pl.pallas_call(
        functools.partial(_mixer_out_body, tiles_per_seq=tiles_per_seq),
        grid=(n // tm,),
        in_specs=[
            row(D_MODEL), row(SB_WIDTH), row(CONV_CH), row(CONV_CH),
            pl.BlockSpec((SUBLANES, CONV_CH), lambda i: (jnp.maximum(i * per8 - 1, 0), 0)),
            pl.BlockSpec((1, SUBLANES, CONV_CH), lambda i: (i // tiles_per_seq, 0, 0)),
            _resident((CONV_K, CONV_CH)), _resident((1, SB_WIDTH)), _resident((1, CONV_CH)),
            _resident(w_out.shape),
        ],
        out_specs=row(D_MODEL),
        out_shape=jax.ShapeDtypeStruct((n, D_MODEL), F32),
        scratch_shapes=[pltpu.VMEM((tm + SUBLANES, CONV_CH), F32)],
        compiler_params=_cparams("parallel"),
        name="mixer_out",
    )(x, attn, bg, u, u, init, w_conv, g_attn, g_conv, w_out)


def _memory_kv_body(m_ref, g_ref, wk_ref, wv_ref, k_ref, v_ref):
    mn = _rms(m_ref[...], g_ref[...]).astype(BF16)
    k_ref[...] = _dot(mn, wk_ref[...])
    v_ref[...] = _dot(mn, wv_ref[...])


def _memory_kv(mem, g, w_ck, w_cv):
    n = mem.shape[0]
    tm = min(ROW_TILE, n)
    row = lambda width: pl.BlockSpec((tm, width), lambda i: (i, 0))
    return pl.pallas_call(
        _memory_kv_body,
        grid=(n // tm,),
        in_specs=[row(D_MODEL), _resident((1, D_MODEL)), _resident(w_ck.shape), _resident(w_cv.shape)],
        out_specs=[row(CA_WIDTH)] * 2,
        out_shape=[jax.ShapeDtypeStruct((n, CA_WIDTH), F32)] * 2,
        compiler_params=_cparams("parallel"),
        name="memory_kv",
    )(mem, g, w_ck, w_cv)


def _cross_body(h_ref, g_ref, wq_ref, mk_ref, mv_ref, wo_ref, o_ref):
    h = h_ref[...]
    q = _dot(_rms(h, g_ref[...]).astype(BF16), wq_ref[...])
    outs = []
    for head in range(CA_HEADS):
        lanes = pl.ds(head * CA_HEAD_DIM, CA_HEAD_DIM)
        qh = q[:, head * CA_HEAD_DIM:(head + 1) * CA_HEAD_DIM].astype(BF16)
        sc = _dot_nt(qh, mk_ref[:, lanes].astype(BF16)) * CA_SCALE
        p = jnp.exp(sc - jnp.max(sc, axis=-1, keepdims=True))
        denom = jnp.sum(p, axis=-1, keepdims=True)
        outs.append(_dot(p.astype(BF16), mv_ref[:, lanes].astype(BF16)) / denom)
    o = jnp.concatenate(outs, axis=-1).astype(BF16)
    o_ref[...] = h + _dot(o, wo_ref[...])


def _cross(h, g, w_cq, mk, mv, w_co, rows_per_batch):
    n = h.shape[0]
    tm = min(ROW_TILE, rows_per_batch)
    tiles_per_batch = rows_per_batch // tm
    row = pl.BlockSpec((tm, D_MODEL), lambda i: (i, 0))
    mem = pl.BlockSpec((N_MEM, CA_WIDTH), lambda i: (i // tiles_per_batch, 0))
    return pl.pallas_call(
        _cross_body,
        grid=(n // tm,),
        in_specs=[row, _resident((1, D_MODEL)), _resident(w_cq.shape), mem, mem,
                  _resident(w_co.shape)],
        out_specs=row,
        out_shape=jax.ShapeDtypeStruct((n, D_MODEL), F32),
        compiler_params=_cparams("parallel"),
        name="cross_attn",
    )(h, g, w_cq, mk, mv, w_co)


def _ffn_body(h_ref, g_ref, wg_ref, wu_ref, wd_ref, gf_ref, y_ref):
    h = h_ref[...]
    hn = _rms(h, g_ref[...]).astype(BF16)
    acc = h
    for c in range(0, wg_ref.shape[1], FFN_CHUNK):
        gate = _dot(hn, wg_ref[:, c:c + FFN_CHUNK])
        up = _dot(hn, wu_ref[:, c:c + FFN_CHUNK])
        act = (gate / (1.0 + jnp.exp(-gate)) * up).astype(BF16)
        acc = acc + _dot(act, wd_ref[c:c + FFN_CHUNK, :])
    y_ref[...] = _rms(acc, gf_ref[...])


def _ffn(h, g, w_gate, w_up, w_down, g_final):
    n = h.shape[0]
    tm = min(ROW_TILE, n)
    row = pl.BlockSpec((tm, D_MODEL), lambda i: (i, 0))
    return pl.pallas_call(
        _ffn_body,
        grid=(n // tm,),
        in_specs=[row, _resident((1, D_MODEL)), _resident(w_gate.shape), _resident(w_up.shape),
                  _resident(w_down.shape), _resident((1, D_MODEL))],
        out_specs=row,
        out_shape=jax.ShapeDtypeStruct((n, D_MODEL), F32),
        compiler_params=_cparams("parallel"),
        name="ffn",
    )(h, g, w_gate, w_up, w_down, g_final)


def kernel(x_prompt, x_sample, mem_prompt, cache_sb_k, cache_sb_v, cache_mem_k, cache_mem_v,
           state_conv, page_table, g_mix, w_in, sb_bias, w_conv, g_attn_out, g_conv_out, w_out,
           g_mem, g_ca, w_cq, w_ck, w_cv, w_co, g_ffn, w_gate, w_up, w_down, g_final):
    depth = w_in.shape[0]
    assert depth == 1, "single-layer stack"
    batch, seq, _ = x_prompt.shape
    dec_batch, dec_seq, _ = x_sample.shape
    assert seq % ROW_TILE == 0 and seq % SB_BLOCK == 0 and dec_seq == SUBLANES
    l = 0
    bf = lambda w: w[l].astype(BF16)
    vec = lambda g: g[l][None, :]
    w_in_b, w_out_b = bf(w_in), bf(w_out)
    w_cq_b, w_ck_b, w_cv_b, w_co_b = bf(w_cq), bf(w_ck), bf(w_cv), bf(w_co)
    w_gate_b, w_up_b, w_down_b = bf(w_gate), bf(w_up), bf(w_down)
    g_fin = g_final[None, :]
    heads = lambda t, b, s: t.reshape(1, b, s, SB_HEADS, SB_HEAD_DIM)

    def tail(x2, attn, bg, u, init, mk, mv, seq_len, rows_per_batch):
        h = _mixer_out(x2, attn, bg, u, init, w_conv[l], vec(g_attn_out), vec(g_conv_out),
                       w_out_b, seq_len)
        h = _cross(h, vec(g_ca), w_cq_b, mk, mv, w_co_b, rows_per_batch)
        return _ffn(h, vec(g_ffn), w_gate_b, w_up_b, w_down_b, g_fin)

    xp = x_prompt.reshape(batch * seq, D_MODEL)
    q, k, v, kb, vb, bg, u = _in_proj(xp, vec(g_mix), w_in_b)
    attn = _sb_prompt(q, kb, vb, sb_bias[l], batch, seq)
    mk, mv = _memory_kv(mem_prompt.reshape(batch * N_MEM, D_MODEL), vec(g_mem), w_ck_b, w_cv_b)
    init_p = jnp.zeros((batch, SUBLANES, CONV_CH), F32)
    y_prompt = tail(xp, attn, bg, u, init_p, mk, mv, seq, seq).reshape(batch, seq, D_MODEL)
    conv_p = u.reshape(batch, seq, CONV_CH)[:, seq - (CONV_K - 1):][None]
    mem_shape = (1, batch, N_MEM, CA_HEADS, CA_HEAD_DIM)

    xs = x_sample.reshape(dec_batch * dec_seq, D_MODEL)
    qs, ks, vs, _, _, bgs, us = _in_proj(xs, vec(g_mix), w_in_b)
    n_pool = cache_sb_k.shape[1]
    attn_s = _sb_sample(qs.astype(F32), ks, vs,
                        cache_sb_k[l].reshape(n_pool, PAGE_SIZE, SB_WIDTH),
                        cache_sb_v[l].reshape(n_pool, PAGE_SIZE, SB_WIDTH),
                        page_table, sb_bias[l], dec_batch, dec_seq)
    init_s = jnp.pad(state_conv[l], ((0, 0), (SUBLANES - (CONV_K - 1), 0), (0, 0)))
    y_sample = tail(xs, attn_s, bgs, us, init_s,
                    cache_mem_k[l].reshape(dec_batch * N_MEM, CA_WIDTH),
                    cache_mem_v[l].reshape(dec_batch * N_MEM, CA_WIDTH),
                    dec_seq, dec_seq).reshape(dec_batch, dec_seq, D_MODEL)
    conv_s = us.reshape(dec_batch, dec_seq, CONV_CH)[:, dec_seq - (CONV_K - 1):][None]

    return (y_prompt, y_sample, heads(k, batch, seq), heads(v, batch, seq), conv_p,
            mk.reshape(mem_shape), mv.reshape(mem_shape),
            heads(ks, dec_batch, dec_seq), heads(vs, dec_batch, dec_seq), conv_s)
```

```python
import functools

import jax
import jax.numpy as jnp
from jax import lax
from jax.experimental import pallas as pl
from jax.experimental.pallas import tpu as pltpu

F32 = jnp.float32
BF16 = jnp.bfloat16

D_MODEL = 1024
SB_HEAD_DIM = 64
SB_WIDTH = 512
SB_HEADS = SB_WIDTH // SB_HEAD_DIM
SB_SCALE = SB_HEAD_DIM ** -0.5
CONV_CH = 512
CONV_K = 3
N_MEM = 256
CA_HEADS = 4
CA_HEAD_DIM = 128
CA_WIDTH = CA_HEADS * CA_HEAD_DIM
CA_SCALE = CA_HEAD_DIM ** -0.5
PAGE_SIZE = 128
EPS = 1e-6
LOG2E = 1.4426950408889634

LANES = 128
SUBLANES = 8
MXU_DIM = 256
VMEM_LIMIT_BYTES = 56 * 1024 * 1024

ROW_TILE = 512
SB_BLOCK = 256
FFN_CHUNK = 256
PAGES_PER_STEP = 8


def _cparams(*sem):
    return pltpu.CompilerParams(dimension_semantics=sem, vmem_limit_bytes=VMEM_LIMIT_BYTES)


def _resident(shape):
    return pl.BlockSpec(shape, lambda *_: (0,) * len(shape), pipeline_mode=pl.Buffered(1))


def _rms(x, g):
    ms = jnp.mean(x * x, axis=-1, keepdims=True)
    return x * lax.rsqrt(ms + EPS) * g


def _dot(a, b):
    return jnp.dot(a, b, preferred_element_type=F32)


def _dot_nt(a, b):
    return lax.dot_general(a, b, (((1,), (1,)), ((), ())), preferred_element_type=F32)


def _log_sigmoids(z):
    soft = jnp.log(1.0 + jnp.exp2(jnp.abs(z) * -LOG2E))
    log_beta = jnp.minimum(z, 0.0) - soft
    return log_beta, log_beta - z


def _suffix_sums(log_om, tmat):
    return _dot(log_om.astype(BF16), tmat)


def _strict_lower_ones(n):
    row = lax.broadcasted_iota(jnp.int32, (n, n), 0)
    col = lax.broadcasted_iota(jnp.int32, (n, n), 1)
    return (row > col).astype(BF16)


def _in_proj_body(x_ref, g_ref, w_ref, q_ref, k_ref, v_ref, kb_ref, vb_ref, bg_ref, u_ref):
    hn = _rms(x_ref[...], g_ref[...]).astype(BF16)

    def proj(c):
        return _dot(hn, w_ref[:, c * SB_WIDTH:(c + 1) * SB_WIDTH])

    q_ref[...] = (proj(0) * SB_SCALE).astype(BF16)
    k = proj(1)
    k_ref[...] = k
    kb_ref[...] = k.astype(BF16)
    v = proj(2)
    v_ref[...] = v
    vb_ref[...] = v.astype(BF16)
    bg_ref[...] = proj(3)
    u_ref[...] = proj(4) * proj(5)


def _in_proj(x, g, w):
    n = x.shape[0]
    tm = min(ROW_TILE, n)
    row = lambda width: pl.BlockSpec((tm, width), lambda i: (i, 0))
    out = lambda dt: jax.ShapeDtypeStruct((n, SB_WIDTH), dt)
    return pl.pallas_call(
        _in_proj_body,
        grid=(n // tm,),
        in_specs=[row(D_MODEL), _resident((1, D_MODEL)), _resident(w.shape)],
        out_specs=[row(SB_WIDTH)] * 7,
        out_shape=[out(BF16), out(F32), out(F32), out(BF16), out(BF16), out(F32), out(F32)],
        compiler_params=_cparams("parallel"),
        name="in_proj",
    )(x, g, w)


def _sb_prompt_body(bias_ref, q_ref, k_ref, v_ref, o_ref, qm_ref, acc_ref, carry_ref):
    blk = SB_BLOCK
    n_pairs = SB_HEADS // 2
    i = pl.program_id(1)
    row = lax.broadcasted_iota(jnp.int32, (2 * blk, blk), 0)
    col = lax.broadcasted_iota(jnp.int32, (2 * blk, blk), 1)
    readable = col < (row & (blk - 1))
    top = lax.broadcasted_iota(jnp.int32, (2 * blk, 1), 0) < blk
    tmat = _strict_lower_ones(blk)
    left = lax.broadcasted_iota(jnp.int32, (blk, LANES), 1) < SB_HEAD_DIM

    for pair in range(n_pairs):
        q2 = q_ref[:, pl.ds(pair * LANES, LANES)]
        zero = jnp.zeros_like(q2)
        qm_ref[pair] = jnp.concatenate(
            [jnp.where(left, q2, zero), jnp.where(left, zero, q2)], axis=0)

    def block(j, diagonal):
        rows = pl.ds(pl.multiple_of(j * blk, blk), blk)
        k_blk, v_blk = k_ref[rows, :], v_ref[rows, :]
        carries = None if diagonal else [carry_ref[pair] for pair in range(n_pairs)]
        pairs = range(n_pairs)
        lanes = [slice(pair * LANES, (pair + 1) * LANES) for pair in pairs]
        qk = [_dot_nt(qm_ref[pair], k_blk[:, lanes[pair]]) for pair in pairs]
        log_beta, log_om = [], []
        for pair in pairs:
            bias = jnp.where(top, bias_ref[2 * pair], bias_ref[2 * pair + 1])
            lb, lo = _log_sigmoids(qk[pair] + bias)
            log_beta.append(lb)
            log_om.append(jnp.where(readable, lo, 0.0) if diagonal else lo)
        suffix = [_suffix_sums(log_om[pair], tmat) for pair in pairs]
        if diagonal:
            a = [jnp.where(readable, jnp.exp(log_beta[pair] + suffix[pair]), 0.0) for pair in pairs]
        else:
            a = [jnp.exp(log_beta[pair] + suffix[pair] + carries[pair]) for pair in pairs]
        outs = [_dot(a[pair].astype(BF16), v_blk[:, lanes[pair]]) for pair in pairs]
        totals = [jnp.sum(log_om[pair], axis=1, keepdims=True) for pair in pairs]
        for pair in pairs:
            if diagonal:
                acc_ref[pair] = outs[pair]
                carry_ref[pair] = totals[pair]
            else:
                acc_ref[pair] += outs[pair]
                carry_ref[pair] = carries[pair] + totals[pair]

    block(i, True)

    @pl.loop(0, i)
    def _(jj):
        block(i - 1 - jj, False)

    for pair in range(n_pairs):
        o = acc_ref[pair]
        o_ref[:, pl.ds(pair * LANES, LANES)] = jnp.where(left, o[:blk], o[blk:])


def _sb_prompt(q, kb, vb, bias, batch, seq):
    nq = seq // SB_BLOCK
    n_pairs = SB_HEADS // 2
    return pl.pallas_call(
        _sb_prompt_body,
        grid=(batch, nq),
        in_specs=[
            pl.BlockSpec(memory_space=pltpu.SMEM),
            pl.BlockSpec((SB_BLOCK, SB_WIDTH), lambda b, i: (b * nq + i, 0)),
            pl.BlockSpec((seq, SB_WIDTH), lambda b, i: (b, 0)),
            pl.BlockSpec((seq, SB_WIDTH), lambda b, i: (b, 0)),
        ],
        out_specs=pl.BlockSpec((SB_BLOCK, SB_WIDTH), lambda b, i: (b * nq + i, 0)),
        out_shape=jax.ShapeDtypeStruct((batch * seq, SB_WIDTH), F32),
        scratch_shapes=[pltpu.VMEM((n_pairs, 2 * SB_BLOCK, LANES), BF16),
                        pltpu.VMEM((n_pairs, 2 * SB_BLOCK, LANES), F32),
                        pltpu.VMEM((n_pairs, 2 * SB_BLOCK, 1), F32)],
        compiler_params=_cparams("parallel", "arbitrary"),
        name="sb_prompt",
    )(bias, q, kb, vb)


def _sb_sample_body(pt_ref, q_ref, kn_ref, vn_ref, brow_ref, *refs, t_new):
    del pt_ref
    pps = PAGES_PER_STEP
    k_refs, v_refs = refs[:pps], refs[pps:2 * pps]
    o_ref, carry_ref, acc_ref = refs[2 * pps:]
    rows = SB_HEADS * t_new
    heads = range(SB_HEADS)
    j = pl.program_id(1)
    tmat = _strict_lower_ones(PAGE_SIZE)
    q = q_ref[...]
    q_heads = [q[:, h * SB_HEAD_DIM:(h + 1) * SB_HEAD_DIM].astype(BF16) for h in heads]

    def pages(k_pages, v_pages, readable):
        n = len(k_pages)
        of_head = lambda ref, h: ref[pl.ds(h, PAGE_SIZE, stride=SB_HEADS), :]
        z = jnp.concatenate(
            [jnp.concatenate([_dot_nt(q_heads[h], of_head(k, h).astype(BF16)) for k in k_pages], axis=1)
             for h in heads], axis=0) + brow_ref[:, :n * PAGE_SIZE]
        log_beta, log_om = _log_sigmoids(z)
        if readable is not None:
            log_om = jnp.where(readable, log_om, 0.0)
        carry = carry_ref[...]
        a = []
        for s in range(n):
            cols = slice(s * PAGE_SIZE, (s + 1) * PAGE_SIZE)
            a_s = jnp.exp(log_beta[:, cols] + _suffix_sums(log_om[:, cols], tmat) + carry)
            a.append(a_s if readable is None else jnp.where(readable, a_s, 0.0))
            carry = carry + jnp.sum(log_om[:, cols], axis=1, keepdims=True)
        carry_ref[...] = carry
        a = jnp.concatenate(a, axis=1)
        for h in heads:
            v_h = jnp.concatenate([of_head(v, h) for v in v_pages], axis=0).astype(BF16)
            acc_ref[h] += _dot(a[h * t_new:(h + 1) * t_new].astype(BF16), v_h)

    @pl.when(j == 0)
    def _():
        carry_ref[...] = jnp.zeros_like(carry_ref)
        acc_ref[...] = jnp.zeros_like(acc_ref)
        key = lax.broadcasted_iota(jnp.int32, (rows, PAGE_SIZE), 1)
        t = lax.broadcasted_iota(jnp.int32, (rows, PAGE_SIZE), 0) % t_new
        pages([kn_ref], [vn_ref], key < t)

    pages(k_refs, v_refs, None)

    @pl.when(j == pl.num_programs(1) - 1)
    def _():
        o_ref[...] = acc_ref[...]


def _sb_sample(q, k_new, v_new, cache_k, cache_v, page_table, bias, batch, t_new):
    n_pages = page_table.shape[1]
    pps = PAGES_PER_STEP
    rows = SB_HEADS * t_new
    brow = jnp.broadcast_to(jnp.repeat(bias, t_new)[:, None], (rows, pps * PAGE_SIZE))
    page_rows = PAGE_SIZE * SB_HEADS
    page_block = (None, page_rows, SB_HEAD_DIM)
    as_rows = lambda c: c.reshape(c.shape[0], page_rows, SB_HEAD_DIM)

    def as_page(x):
        x = x.reshape(batch, t_new * SB_HEADS, SB_HEAD_DIM)
        return jnp.pad(x, ((0, 0), (0, page_rows - t_new * SB_HEADS), (0, 0)))

    def page_spec(s):
        def index(b, j, pt):
            logical = n_pages - 1 - (j * pps + s)
            return (pt[b * n_pages + logical], 0, 0)
        return pl.BlockSpec(page_block, index)

    new_spec = pl.BlockSpec(page_block, lambda b, j, pt: (b, 0, 0))
    out_block = (None, SB_HEADS, t_new, SB_HEAD_DIM)
    grid_spec = pltpu.PrefetchScalarGridSpec(
        num_scalar_prefetch=1,
        grid=(batch, n_pages // pps),
        in_specs=[pl.BlockSpec((t_new, SB_WIDTH), lambda b, j, pt: (b, 0)), new_spec, new_spec,
                  pl.BlockSpec(brow.shape, lambda b, j, pt: (0, 0))]
                 + [page_spec(s) for s in range(pps)] * 2,
        out_specs=pl.BlockSpec(out_block, lambda b, j, pt: (b, 0, 0, 0)),
        scratch_shapes=[pltpu.VMEM((rows, PAGE_SIZE), F32),
                        pltpu.VMEM((SB_HEADS, t_new, SB_HEAD_DIM), F32)],
    )
    o = pl.pallas_call(
        functools.partial(_sb_sample_body, t_new=t_new),
        grid_spec=grid_spec,
        out_shape=jax.ShapeDtypeStruct((batch, SB_HEADS, t_new, SB_HEAD_DIM), F32),
        compiler_params=_cparams("parallel", "arbitrary"),
        name="sb_sample",
    )(page_table.reshape(-1), q, as_page(k_new), as_page(v_new), brow,
      *([as_rows(cache_k)] * pps), *([as_rows(cache_v)] * pps))
    return o.transpose(0, 2, 1, 3).reshape(batch * t_new, SB_WIDTH)


def _mixer_out_body(x_ref, a_ref, bg_ref, u_ref, uprev_ref, init_ref, wc_ref, ga_ref, gc_ref,
                    wo_ref, h_ref, ext_ref, *, tiles_per_seq):
    tm = u_ref.shape[0]
    i = pl.program_id(0)
    if tiles_per_seq == 1:
        halo = init_ref[0]
    else:
        halo = jnp.where(lax.rem(i, tiles_per_seq) == 0, init_ref[0], uprev_ref[...])
    u = u_ref[...]
    ext_ref[0:SUBLANES, :] = halo
    ext_ref[SUBLANES:SUBLANES + tm, :] = u
    u1 = ext_ref[SUBLANES - 1:SUBLANES - 1 + tm, :]
    u2 = ext_ref[SUBLANES - 2:SUBLANES - 2 + tm, :]
    wc = wc_ref[...]
    conv = wc[0:1] * u2 + wc[1:2] * u1 + wc[2:3] * u
    c = _rms(bg_ref[...] * conv, gc_ref[...]).astype(BF16)
    a = _rms(a_ref[...], ga_ref[...]).astype(BF16)
    h_ref[...] = (x_ref[...] + _dot(a, wo_ref[0:SB_WIDTH, :])
                  + _dot(c, wo_ref[SB_WIDTH:SB_WIDTH + CONV_CH, :]))


def _mixer_out(x, attn, bg, u, init, w_conv, g_attn, g_conv, w_out, seq):
    n = x.shape[0]
    tm = min(ROW_TILE, seq)
    tiles_per_seq = seq // tm
    per8 = tm // SUBLANES
    row = lambda width: pl.BlockSpec((tm, width), lambda i: (i, 0))
    return pl.pallas_call(
        functools.partial(_mixer_out_body, tiles_per_seq=tiles_per_seq),
        grid=(n // tm,),
        in_specs=[
            row(D_MODEL), row(SB_WIDTH), row(CONV_CH), row(CONV_CH),
            pl.BlockSpec((SUBLANES, CONV_CH), lambda i: (jnp.maximum(i * per8 - 1, 0), 0)),
            pl.BlockSpec((1, SUBLANES, CONV_CH), lambda i: (i // tiles_per_seq, 0, 0)),
            _resident((CONV_K, CONV_CH)), _resident((1, SB_WIDTH)), _resident((1, CONV_CH)),
            _resident(w_out.shape),
        ],
        out_specs=row(D_MODEL),
        out_shape=jax.ShapeDtypeStruct((n, D_MODEL), F32),
        scratch_shapes=[pltpu.VMEM((tm + SUBLANES, CONV_CH), F32)],
        compiler_params=_cparams("parallel"),
        name="mixer_out",
    )(x, attn, bg, u, u, init, w_conv, g_attn, g_conv, w_out)


def _memory_kv_body(m_ref, g_ref, wk_ref, wv_ref, k_ref, v_ref):
    mn = _rms(m_ref[...], g_ref[...]).astype(BF16)
    k_ref[...] = _dot(mn, wk_ref[...])
    v_ref[...] = _dot(mn, wv_ref[...])


def _memory_kv(mem, g, w_ck, w_cv):
    n = mem.shape[0]
    tm = min(ROW_TILE, n)
    row = lambda width: pl.BlockSpec((tm, width), lambda i: (i, 0))
    return pl.pallas_call(
        _memory_kv_body,
        grid=(n // tm,),
        in_specs=[row(D_MODEL), _resident((1, D_MODEL)), _resident(w_ck.shape), _resident(w_cv.shape)],
        out_specs=[row(CA_WIDTH)] * 2,
        out_shape=[jax.ShapeDtypeStruct((n, CA_WIDTH), F32)] * 2,
        compiler_params=_cparams("parallel"),
        name="memory_kv",
    )(mem, g, w_ck, w_cv)


def _cross_body(h_ref, g_ref, wq_ref, mk_ref, mv_ref, wo_ref, o_ref):
    h = h_ref[...]
    q = _dot(_rms(h, g_ref[...]).astype(BF16), wq_ref[...])
    outs = []
    for head in range(CA_HEADS):
        lanes = pl.ds(head * CA_HEAD_DIM, CA_HEAD_DIM)
        qh = q[:, head * CA_HEAD_DIM:(head + 1) * CA_HEAD_DIM].astype(BF16)
        sc = _dot_nt(qh, mk_ref[:, lanes].astype(BF16)) * CA_SCALE
        p = jnp.exp(sc - jnp.max(sc, axis=-1, keepdims=True))
        denom = jnp.sum(p, axis=-1, keepdims=True)
        outs.append(_dot(p.astype(BF16), mv_ref[:, lanes].astype(BF16)) / denom)
    o = jnp.concatenate(outs, axis=-1).astype(BF16)
    o_ref[...] = h + _dot(o, wo_ref[...])


def _cross(h, g, w_cq, mk, mv, w_co, rows_per_batch):
    n = h.shape[0]
    tm = min(ROW_TILE, rows_per_batch)
    tiles_per_batch = rows_per_batch // tm
    row = pl.BlockSpec((tm, D_MODEL), lambda i: (i, 0))
    mem = pl.BlockSpec((N_MEM, CA_WIDTH), lambda i: (i // tiles_per_batch, 0))
    return pl.pallas_call(
        _cross_body,
        grid=(n // tm,),
        in_specs=[row, _resident((1, D_MODEL)), _resident(w_cq.shape), mem, mem,
                  _resident(w_co.shape)],
        out_specs=row,
        out_shape=jax.ShapeDtypeStruct((n, D_MODEL), F32),
        compiler_params=_cparams("parallel"),
        name="cross_attn",
    )(h, g, w_cq, mk, mv, w_co)


def _ffn_body(h_ref, g_ref, wg_ref, wu_ref, wd_ref, gf_ref, y_ref):
    h = h_ref[...]
    hn = _rms(h, g_ref[...]).astype(BF16)
    acc = h
    for c in range(0, wg_ref.shape[1], FFN_CHUNK):
        gate = _dot(hn, wg_ref[:, c:c + FFN_CHUNK])
        up = _dot(hn, wu_ref[:, c:c + FFN_CHUNK])
        act = (gate / (1.0 + jnp.exp(-gate)) * up).astype(BF16)
        acc = acc + _dot(act, wd_ref[c:c + FFN_CHUNK, :])
    y_ref[...] = _rms(acc, gf_ref[...])


def _ffn(h, g, w_gate, w_up, w_down, g_final):
    n = h.shape[0]
    tm = min(ROW_TILE, n)
    row = pl.BlockSpec((tm, D_MODEL), lambda i: (i, 0))
    return pl.pallas_call(
        _ffn_body,
        grid=(n // tm,),
        in_specs=[row, _resident((1, D_MODEL)), _resident(w_gate.shape), _resident(w_up.shape),
                  _resident(w_down.shape), _resident((1, D_MODEL))],
        out_specs=row,
        out_shape=jax.ShapeDtypeStruct((n, D_MODEL), F32),
        compiler_params=_cparams("parallel"),
        name="ffn",
    )(h, g, w_gate, w_up, w_down, g_final)


def kernel(x_prompt, x_sample, mem_prompt, cache_sb_k, cache_sb_v, cache_mem_k, cache_mem_v,
           state_conv, page_table, g_mix, w_in, sb_bias, w_conv, g_attn_out, g_conv_out, w_out,
           g_mem, g_ca, w_cq, w_ck, w_cv, w_co, g_ffn, w_gate, w_up, w_down, g_final):
    depth = w_in.shape[0]
    assert depth == 1, "single-layer stack"
    batch, seq, _ = x_prompt.shape
    dec_batch, dec_seq, _ = x_sample.shape
    assert seq % ROW_TILE == 0 and seq % SB_BLOCK == 0 and dec_seq == SUBLANES
    l = 0
    bf = lambda w: w[l].astype(BF16)
    vec = lambda g: g[l][None, :]
    w_in_b, w_out_b = bf(w_in), bf(w_out)
    w_cq_b, w_ck_b, w_cv_b, w_co_b = bf(w_cq), bf(w_ck), bf(w_cv), bf(w_co)
    w_gate_b, w_up_b, w_down_b = bf(w_gate), bf(w_up), bf(w_down)
    g_fin = g_final[None, :]
    heads = lambda t, b, s: t.reshape(1, b, s, SB_HEADS, SB_HEAD_DIM)

    def tail(x2, attn, bg, u, init, mk, mv, seq_len, rows_per_batch):
        h = _mixer_out(x2, attn, bg, u, init, w_conv[l], vec(g_attn_out), vec(g_conv_out),
                       w_out_b, seq_len)
        h = _cross(h, vec(g_ca), w_cq_b, mk, mv, w_co_b, rows_per_batch)
        return _ffn(h, vec(g_ffn), w_gate_b, w_up_b, w_down_b, g_fin)

    xp = x_prompt.reshape(batch * seq, D_MODEL)
    q, k, v, kb, vb, bg, u = _in_proj(xp, vec(g_mix), w_in_b)
    attn = _sb_prompt(q, kb, vb, sb_bias[l], batch, seq)
    mk, mv = _memory_kv(mem_prompt.reshape(batch * N_MEM, D_MODEL), vec(g_mem), w_ck_b, w_cv_b)
    init_p = jnp.zeros((batch, SUBLANES, CONV_CH), F32)
    y_prompt = tail(xp, attn, bg, u, init_p, mk, mv, seq, seq).reshape(batch, seq, D_MODEL)
    conv_p = u.reshape(batch, seq, CONV_CH)[:, seq - (CONV_K - 1):][None]
    mem_shape = (1, batch, N_MEM, CA_HEADS, CA_HEAD_DIM)

    xs = x_sample.reshape(dec_batch * dec_seq, D_MODEL)
    qs, ks, vs, _, _, bgs, us = _in_proj(xs, vec(g_mix), w_in_b)
    attn_s = _sb_sample(qs.astype(F32), ks, vs, cache_sb_k[l], cache_sb_v[l],
                        page_table, sb_bias[l], dec_batch, dec_seq)
    init_s = jnp.pad(state_conv[l], ((0, 0), (SUBLANES - (CONV_K - 1), 0), (0, 0)))
    y_sample = tail(xs, attn_s, bgs, us, init_s,
                    cache_mem_k[l].reshape(dec_batch * N_MEM, CA_WIDTH),
                    cache_mem_v[l].reshape(dec_batch * N_MEM, CA_WIDTH),
                    dec_seq, dec_seq).reshape(dec_batch, dec_seq, D_MODEL)
    conv_s = us.reshape(dec_batch, dec_seq, CONV_CH)[:, dec_seq - (CONV_K - 1):][None]

    return (y_prompt, y_sample, heads(k, batch, seq), heads(v, batch, seq), conv_p,
            mk.reshape(mem_shape), mv.reshape(mem_shape),
            heads(ks, dec_batch, dec_seq), heads(vs, dec_batch, dec_seq), conv_s)
```

```python
import functools

import jax
import jax.numpy as jnp
from jax import lax
from jax.experimental import pallas as pl
from jax.experimental.pallas import tpu as pltpu

F32 = jnp.float32
BF16 = jnp.bfloat16

D_MODEL = 1024
SB_HEAD_DIM = 64
SB_WIDTH = 512
SB_HEADS = SB_WIDTH // SB_HEAD_DIM
SB_SCALE = SB_HEAD_DIM ** -0.5
CONV_CH = 512
CONV_K = 3
N_MEM = 256
CA_HEADS = 4
CA_HEAD_DIM = 128
CA_WIDTH = CA_HEADS * CA_HEAD_DIM
CA_SCALE = CA_HEAD_DIM ** -0.5
PAGE_SIZE = 128
EPS = 1e-6
LOG2E = 1.4426950408889634

LANES = 128
SUBLANES = 8
MXU_DIM = 256
VMEM_LIMIT_BYTES = 56 * 1024 * 1024

ROW_TILE = 512
SB_BLOCK = 256
FFN_CHUNK = 256
PAGES_PER_STEP = 8


def _cparams(*sem):
    return pltpu.CompilerParams(dimension_semantics=sem, vmem_limit_bytes=VMEM_LIMIT_BYTES)


def _resident(shape):
    return pl.BlockSpec(shape, lambda *_: (0,) * len(shape), pipeline_mode=pl.Buffered(1))


def _rms(x, g):
    ms = jnp.mean(x * x, axis=-1, keepdims=True)
    return x * lax.rsqrt(ms + EPS) * g


def _dot(a, b):
    return jnp.dot(a, b, preferred_element_type=F32)


def _dot_nt(a, b):
    return lax.dot_general(a, b, (((1,), (1,)), ((), ())), preferred_element_type=F32)


def _log_sigmoids(z):
    soft = jnp.log(1.0 + jnp.exp2(jnp.abs(z) * -LOG2E))
    log_beta = jnp.minimum(z, 0.0) - soft
    return log_beta, log_beta - z


def _suffix_sums(log_om, tmat):
    return _dot(log_om.astype(BF16), tmat)


def _strict_lower_ones(n):
    row = lax.broadcasted_iota(jnp.int32, (n, n), 0)
    col = lax.broadcasted_iota(jnp.int32, (n, n), 1)
    return (row > col).astype(BF16)


def _in_proj_body(x_ref, g_ref, w_ref, q_ref, k_ref, v_ref, kb_ref, vb_ref, bg_ref, u_ref):
    hn = _rms(x_ref[...], g_ref[...]).astype(BF16)

    def proj(c):
        return _dot(hn, w_ref[:, c * SB_WIDTH:(c + 1) * SB_WIDTH])

    q_ref[...] = (proj(0) * SB_SCALE).astype(BF16)
    k = proj(1)
    k_ref[...] = k
    kb_ref[...] = k.astype(BF16)
    v = proj(2)
    v_ref[...] = v
    vb_ref[...] = v.astype(BF16)
    bg_ref[...] = proj(3)
    u_ref[...] = proj(4) * proj(5)


def _in_proj(x, g, w):
    n = x.shape[0]
    tm = min(ROW_TILE, n)
    row = lambda width: pl.BlockSpec((tm, width), lambda i: (i, 0))
    out = lambda dt: jax.ShapeDtypeStruct((n, SB_WIDTH), dt)
    return pl.pallas_call(
        _in_proj_body,
        grid=(n // tm,),
        in_specs=[row(D_MODEL), _resident((1, D_MODEL)), _resident(w.shape)],
        out_specs=[row(SB_WIDTH)] * 7,
        out_shape=[out(BF16), out(F32), out(F32), out(BF16), out(BF16), out(F32), out(F32)],
        compiler_params=_cparams("parallel"),
        name="in_proj",
    )(x, g, w)


def _sb_prompt_body(bias_ref, q_ref, k_ref, v_ref, o_ref, qm_ref, acc_ref, carry_ref):
    blk = SB_BLOCK
    n_pairs = SB_HEADS // 2
    i = pl.program_id(1)
    row = lax.broadcasted_iota(jnp.int32, (2 * blk, blk), 0)
    col = lax.broadcasted_iota(jnp.int32, (2 * blk, blk), 1)
    readable = col < (row & (blk - 1))
    top = lax.broadcasted_iota(jnp.int32, (2 * blk, 1), 0) < blk
    tmat = _strict_lower_ones(blk)
    left = lax.broadcasted_iota(jnp.int32, (blk, LANES), 1) < SB_HEAD_DIM

    for pair in range(n_pairs):
        q2 = q_ref[:, pl.ds(pair * LANES, LANES)]
        zero = jnp.zeros_like(q2)
        qm_ref[pair] = jnp.concatenate(
            [jnp.where(left, q2, zero), jnp.where(left, zero, q2)], axis=0)

    def block(j, diagonal):
        rows = pl.ds(pl.multiple_of(j * blk, blk), blk)
        k_blk, v_blk = k_ref[rows, :], v_ref[rows, :]
        carries = None if diagonal else [carry_ref[pair] for pair in range(n_pairs)]
        pairs = range(n_pairs)
        lanes = [slice(pair * LANES, (pair + 1) * LANES) for pair in pairs]
        qk = [_dot_nt(qm_ref[pair], k_blk[:, lanes[pair]]) for pair in pairs]
        log_beta, log_om = [], []
        for pair in pairs:
            bias = jnp.where(top, bias_ref[2 * pair], bias_ref[2 * pair + 1])
            lb, lo = _log_sigmoids(qk[pair] + bias)
            log_beta.append(lb)
            log_om.append(jnp.where(readable, lo, 0.0) if diagonal else lo)
        suffix = [_suffix_sums(log_om[pair], tmat) for pair in pairs]
        if diagonal:
            a = [jnp.where(readable, jnp.exp(log_beta[pair] + suffix[pair]), 0.0) for pair in pairs]
        else:
            a = [jnp.exp(log_beta[pair] + suffix[pair] + carries[pair]) for pair in pairs]
        outs = [_dot(a[pair].astype(BF16), v_blk[:, lanes[pair]]) for pair in pairs]
        totals = [jnp.sum(log_om[pair], axis=1, keepdims=True) for pair in pairs]
        for pair in pairs:
            if diagonal:
                acc_ref[pair] = outs[pair]
                carry_ref[pair] = totals[pair]
            else:
                acc_ref[pair] += outs[pair]
                carry_ref[pair] = carries[pair] + totals[pair]

    block(i, True)

    @pl.loop(0, i)
    def _(jj):
        block(i - 1 - jj, False)

    for pair in range(n_pairs):
        o = acc_ref[pair]
        o_ref[:, pl.ds(pair * LANES, LANES)] = jnp.where(left, o[:blk], o[blk:])


def _sb_prompt(q, kb, vb, bias, batch, seq):
    nq = seq // SB_BLOCK
    n_pairs = SB_HEADS // 2
    return pl.pallas_call(
        _sb_prompt_body,
        grid=(batch, nq),
        in_specs=[
            pl.BlockSpec(memory_space=pltpu.SMEM),
            pl.BlockSpec((SB_BLOCK, SB_WIDTH), lambda b, i: (b * nq + i, 0)),
            pl.BlockSpec((seq, SB_WIDTH), lambda b, i: (b, 0)),
            pl.BlockSpec((seq, SB_WIDTH), lambda b, i: (b, 0)),
        ],
        out_specs=pl.BlockSpec((SB_BLOCK, SB_WIDTH), lambda b, i: (b * nq + i, 0)),
        out_shape=jax.ShapeDtypeStruct((batch * seq, SB_WIDTH), F32),
        scratch_shapes=[pltpu.VMEM((n_pairs, 2 * SB_BLOCK, LANES), BF16),
                        pltpu.VMEM((n_pairs, 2 * SB_BLOCK, LANES), F32),
                        pltpu.VMEM((n_pairs, 2 * SB_BLOCK, 1), F32)],
        compiler_params=_cparams("parallel", "arbitrary"),
        name="sb_prompt",
    )(bias, q, kb, vb)


def _sb_sample_body(pt_ref, q_ref, kn_ref, vn_ref, brow_ref, *refs, t_new):
    del pt_ref
    pps = PAGES_PER_STEP
    k_refs, v_refs = refs[:pps], refs[pps:2 * pps]
    o_ref, qbd_ref, carry_ref, acc_ref = refs[2 * pps:]
    rows = SB_HEADS * t_new
    j = pl.program_id(1)
    tmat = _strict_lower_ones(PAGE_SIZE)
    row_head = lax.broadcasted_iota(jnp.int32, (rows, SB_WIDTH), 0) // t_new
    col_head = lax.broadcasted_iota(jnp.int32, (rows, SB_WIDTH), 1) // SB_HEAD_DIM

    def pages(k_pages, v_pages, readable):
        n = len(k_pages)
        kt = jnp.concatenate([k[...] for k in k_pages], axis=1).astype(BF16)
        vt = jnp.concatenate([v[...] for v in v_pages], axis=1).astype(BF16)
        z = _dot(qbd_ref[...], kt) + brow_ref[:, :n * PAGE_SIZE]
        log_beta, log_om = _log_sigmoids(z)
        if readable is not None:
            log_om = jnp.where(readable, log_om, 0.0)
        carry = carry_ref[...]
        a = []
        for s in range(n):
            cols = slice(s * PAGE_SIZE, (s + 1) * PAGE_SIZE)
            a_s = jnp.exp(log_beta[:, cols] + _suffix_sums(log_om[:, cols], tmat) + carry)
            a.append(a_s if readable is None else jnp.where(readable, a_s, 0.0))
            carry = carry + jnp.sum(log_om[:, cols], axis=1, keepdims=True)
        carry_ref[...] = carry
        acc_ref[...] += _dot_nt(jnp.concatenate(a, axis=1).astype(BF16), vt)

    @pl.when(j == 0)
    def _():
        q_rows = jnp.concatenate([q_ref[...]] * SB_HEADS, axis=0)
        qbd_ref[...] = jnp.where(row_head == col_head, q_rows, 0.0).astype(BF16)
        carry_ref[...] = jnp.zeros_like(carry_ref)
        acc_ref[...] = jnp.zeros_like(acc_ref)
        key = lax.broadcasted_iota(jnp.int32, (rows, PAGE_SIZE), 1)
        t = lax.broadcasted_iota(jnp.int32, (rows, PAGE_SIZE), 0) % t_new
        pages([kn_ref], [vn_ref], key < t)

    pages(k_refs, v_refs, None)

    @pl.when(j == pl.num_programs(1) - 1)
    def _():
        acc = acc_ref[...]
        o = jnp.zeros((t_new, SB_WIDTH), F32)
        for head in range(SB_HEADS):
            band = acc[head * t_new:(head + 1) * t_new]
            o = o + jnp.where(col_head[:t_new] == head, band, 0.0)
        o_ref[...] = o


def _sb_sample(q, k_new, v_new, cache_k, cache_v, page_table, bias, batch, t_new):
    n_pages = page_table.shape[1]
    pps = PAGES_PER_STEP
    rows = SB_HEADS * t_new
    brow = jnp.broadcast_to(jnp.repeat(bias, t_new)[:, None], (rows, pps * PAGE_SIZE))
    page_block = (None, SB_WIDTH, PAGE_SIZE)
    keys_minor = lambda c: c.transpose(0, 2, 3, 1).reshape(c.shape[0], SB_WIDTH, PAGE_SIZE)

    def as_page(x):
        x = x.reshape(batch, t_new, SB_WIDTH).transpose(0, 2, 1)
        return jnp.pad(x, ((0, 0), (0, 0), (0, PAGE_SIZE - t_new)))

    def page_spec(s):
        def index(b, j, pt):
            logical = n_pages - 1 - (j * pps + s)
            return (pt[b * n_pages + logical], 0, 0)
        return pl.BlockSpec(page_block, index)

    new_spec = pl.BlockSpec(page_block, lambda b, j, pt: (b, 0, 0))
    tok_spec = pl.BlockSpec((t_new, SB_WIDTH), lambda b, j, pt: (b, 0))
    grid_spec = pltpu.PrefetchScalarGridSpec(
        num_scalar_prefetch=1,
        grid=(batch, n_pages // pps),
        in_specs=[tok_spec, new_spec, new_spec, pl.BlockSpec(brow.shape, lambda b, j, pt: (0, 0))]
                 + [page_spec(s) for s in range(pps)] * 2,
        out_specs=tok_spec,
        scratch_shapes=[pltpu.VMEM((rows, SB_WIDTH), BF16),
                        pltpu.VMEM((rows, PAGE_SIZE), F32),
                        pltpu.VMEM((rows, SB_WIDTH), F32)],
    )
    return pl.pallas_call(
        functools.partial(_sb_sample_body, t_new=t_new),
        grid_spec=grid_spec,
        out_shape=jax.ShapeDtypeStruct((batch * t_new, SB_WIDTH), F32),
        compiler_params=_cparams("parallel", "arbitrary"),
        name="sb_sample",
    )(page_table.reshape(-1), q, as_page(k_new), as_page(v_new), brow,
      *([keys_minor(cache_k)] * pps), *([keys_minor(cache_v)] * pps))


def _mixer_out_body(x_ref, a_ref, bg_ref, u_ref, uprev_ref, init_ref, wc_ref, ga_ref, gc_ref,
                    wo_ref, h_ref, ext_ref, *, tiles_per_seq):
    tm = u_ref.shape[0]
    i = pl.program_id(0)
    if tiles_per_seq == 1:
        halo = init_ref[0]
    else:
        halo = jnp.where(lax.rem(i, tiles_per_seq) == 0, init_ref[0], uprev_ref[...])
    u = u_ref[...]
    ext_ref[0:SUBLANES, :] = halo
    ext_ref[SUBLANES:SUBLANES + tm, :] = u
    u1 = ext_ref[SUBLANES - 1:SUBLANES - 1 + tm, :]
    u2 = ext_ref[SUBLANES - 2:SUBLANES - 2 + tm, :]
    wc = wc_ref[...]
    conv = wc[0:1] * u2 + wc[1:2] * u1 + wc[2:3] * u
    c = _rms(bg_ref[...] * conv, gc_ref[...]).astype(BF16)
    a = _rms(a_ref[...], ga_ref[...]).astype(BF16)
    h_ref[...] = (x_ref[...] + _dot(a, wo_ref[0:SB_WIDTH, :])
                  + _dot(c, wo_ref[SB_WIDTH:SB_WIDTH + CONV_CH, :]))


def _mixer_out(x, attn, bg, u, init, w_conv, g_attn, g_conv, w_out, seq):
    n = x.shape[0]
    tm = min(ROW_TILE, seq)
    tiles_per_seq = seq // tm
    per8 = tm // SUBLANES
    row = lambda width: pl.BlockSpec((tm, width), lambda i: (i, 0))
    return pl.pallas_call(
        functools.partial(_mixer_out_body, tiles_per_seq=tiles_per_seq),
        grid=(n // tm,),
        in_specs=[
            row(D_MODEL), row(SB_WIDTH), row(CONV_CH), row(CONV_CH),
            pl.BlockSpec((SUBLANES, CONV_CH), lambda i: (jnp.maximum(i * per8 - 1, 0), 0)),
            pl.BlockSpec((1, SUBLANES, CONV_CH), lambda i: (i // tiles_per_seq, 0, 0)),
            _resident((CONV_K, CONV_CH)), _resident((1, SB_WIDTH)), _resident((1, CONV_CH)),
            _resident(w_out.shape),
        ],
        out_specs=row(D_MODEL),
        out_shape=jax.ShapeDtypeStruct((n, D_MODEL), F32),
        scratch_shapes=[pltpu.VMEM((tm + SUBLANES, CONV_CH), F32)],
        compiler_params=_cparams("parallel"),
        name="mixer_out",
    )(x, attn, bg, u, u, init, w_conv, g_attn, g_conv, w_out)


def _memory_kv_body(m_ref, g_ref, wk_ref, wv_ref, k_ref, v_ref):
    mn = _rms(m_ref[...], g_ref[...]).astype(BF16)
    k_ref[...] = _dot(mn, wk_ref[...])
    v_ref[...] = _dot(mn, wv_ref[...])


def _memory_kv(mem, g, w_ck, w_cv):
    n = mem.shape[0]
    tm = min(ROW_TILE, n)
    row = lambda width: pl.BlockSpec((tm, width), lambda i: (i, 0))
    return pl.pallas_call(
        _memory_kv_body,
        grid=(n // tm,),
        in_specs=[row(D_MODEL), _resident((1, D_MODEL)), _resident(w_ck.shape), _resident(w_cv.shape)],
        out_specs=[row(CA_WIDTH)] * 2,
        out_shape=[jax.ShapeDtypeStruct((n, CA_WIDTH), F32)] * 2,
        compiler_params=_cparams("parallel"),
        name="memory_kv",
    )(mem, g, w_ck, w_cv)


def _cross_body(h_ref, g_ref, wq_ref, mk_ref, mv_ref, wo_ref, o_ref):
    h = h_ref[...]
    q = _dot(_rms(h, g_ref[...]).astype(BF16), wq_ref[...])
    outs = []
    for head in range(CA_HEADS):
        lanes = pl.ds(head * CA_HEAD_DIM, CA_HEAD_DIM)
        qh = q[:, head * CA_HEAD_DIM:(head + 1) * CA_HEAD_DIM].astype(BF16)
        sc = _dot_nt(qh, mk_ref[:, lanes].astype(BF16)) * CA_SCALE
        p = jnp.exp(sc - jnp.max(sc, axis=-1, keepdims=True))
        denom = jnp.sum(p, axis=-1, keepdims=True)
        outs.append(_dot(p.astype(BF16), mv_ref[:, lanes].astype(BF16)) / denom)
    o = jnp.concatenate(outs, axis=-1).astype(BF16)
    o_ref[...] = h + _dot(o, wo_ref[...])


def _cross(h, g, w_cq, mk, mv, w_co, rows_per_batch):
    n = h.shape[0]
    tm = min(ROW_TILE, rows_per_batch)
    tiles_per_batch = rows_per_batch // tm
    row = pl.BlockSpec((tm, D_MODEL), lambda i: (i, 0))
    mem = pl.BlockSpec((N_MEM, CA_WIDTH), lambda i: (i // tiles_per_batch, 0))
    return pl.pallas_call(
        _cross_body,
        grid=(n // tm,),
        in_specs=[row, _resident((1, D_MODEL)), _resident(w_cq.shape), mem, mem,
                  _resident(w_co.shape)],
        out_specs=row,
        out_shape=jax.ShapeDtypeStruct((n, D_MODEL), F32),
        compiler_params=_cparams("parallel"),
        name="cross_attn",
    )(h, g, w_cq, mk, mv, w_co)


def _ffn_body(h_ref, g_ref, wg_ref, wu_ref, wd_ref, gf_ref, y_ref):
    h = h_ref[...]
    hn = _rms(h, g_ref[...]).astype(BF16)
    acc = h
    for c in range(0, wg_ref.shape[1], FFN_CHUNK):
        gate = _dot(hn, wg_ref[:, c:c + FFN_CHUNK])
        up = _dot(hn, wu_ref[:, c:c + FFN_CHUNK])
        act = (gate / (1.0 + jnp.exp(-gate)) * up).astype(BF16)
        acc = acc + _dot(act, wd_ref[c:c + FFN_CHUNK, :])
    y_ref[...] = _rms(acc, gf_ref[...])


def _ffn(h, g, w_gate, w_up, w_down, g_final):
    n = h.shape[0]
    tm = min(ROW_TILE, n)
    row = pl.BlockSpec((tm, D_MODEL), lambda i: (i, 0))
    return pl.pallas_call(
        _ffn_body,
        grid=(n // tm,),
        in_specs=[row, _resident((1, D_MODEL)), _resident(w_gate.shape), _resident(w_up.shape),
                  _resident(w_down.shape), _resident((1, D_MODEL))],
        out_specs=row,
        out_shape=jax.ShapeDtypeStruct((n, D_MODEL), F32),
        compiler_params=_cparams("parallel"),
        name="ffn",
    )(h, g, w_gate, w_up, w_down, g_final)


def kernel(x_prompt, x_sample, mem_prompt, cache_sb_k, cache_sb_v, cache_mem_k, cache_mem_v,
           state_conv, page_table, g_mix, w_in, sb_bias, w_conv, g_attn_out, g_conv_out, w_out,
           g_mem, g_ca, w_cq, w_ck, w_cv, w_co, g_ffn, w_gate, w_up, w_down, g_final):
    depth = w_in.shape[0]
    assert depth == 1, "single-layer stack"
    batch, seq, _ = x_prompt.shape
    dec_batch, dec_seq, _ = x_sample.shape
    assert seq % ROW_TILE == 0 and seq % SB_BLOCK == 0 and dec_seq == SUBLANES
    l = 0
    bf = lambda w: w[l].astype(BF16)
    vec = lambda g: g[l][None, :]
    w_in_b, w_out_b = bf(w_in), bf(w_out)
    w_cq_b, w_ck_b, w_cv_b, w_co_b = bf(w_cq), bf(w_ck), bf(w_cv), bf(w_co)
    w_gate_b, w_up_b, w_down_b = bf(w_gate), bf(w_up), bf(w_down)
    g_fin = g_final[None, :]
    heads = lambda t, b, s: t.reshape(1, b, s, SB_HEADS, SB_HEAD_DIM)

    def tail(x2, attn, bg, u, init, mk, mv, seq_len, rows_per_batch):
        h = _mixer_out(x2, attn, bg, u, init, w_conv[l], vec(g_attn_out), vec(g_conv_out),
                       w_out_b, seq_len)
        h = _cross(h, vec(g_ca), w_cq_b, mk, mv, w_co_b, rows_per_batch)
        return _ffn(h, vec(g_ffn), w_gate_b, w_up_b, w_down_b, g_fin)

    xp = x_prompt.reshape(batch * seq, D_MODEL)
    q, k, v, kb, vb, bg, u = _in_proj(xp, vec(g_mix), w_in_b)
    attn = _sb_prompt(q, kb, vb, sb_bias[l], batch, seq)
    mk, mv = _memory_kv(mem_prompt.reshape(batch * N_MEM, D_MODEL), vec(g_mem), w_ck_b, w_cv_b)
    init_p = jnp.zeros((batch, SUBLANES, CONV_CH), F32)
    y_prompt = tail(xp, attn, bg, u, init_p, mk, mv, seq, seq).reshape(batch, seq, D_MODEL)
    conv_p = u.reshape(batch, seq, CONV_CH)[:, seq - (CONV_K - 1):][None]
    mem_shape = (1, batch, N_MEM, CA_HEADS, CA_HEAD_DIM)

    xs = x_sample.reshape(dec_batch * dec_seq, D_MODEL)
    qs, ks, vs, _, _, bgs, us = _in_proj(xs, vec(g_mix), w_in_b)
    attn_s = _sb_sample(qs.astype(F32), ks, vs, cache_sb_k[l], cache_sb_v[l],
                        page_table, sb_bias[l], dec_batch, dec_seq)
    init_s = jnp.pad(state_conv[l], ((0, 0), (SUBLANES - (CONV_K - 1), 0), (0, 0)))
    y_sample = tail(xs, attn_s, bgs, us, init_s,
                    cache_mem_k[l].reshape(dec_batch * N_MEM, CA_WIDTH),
                    cache_mem_v[l].reshape(dec_batch * N_MEM, CA_WIDTH),
                    dec_seq, dec_seq).reshape(dec_batch, dec_seq, D_MODEL)
    conv_s = us.reshape(dec_batch, dec_seq, CONV_CH)[:, dec_seq - (CONV_K - 1):][None]

    return (y_prompt, y_sample, heads(k, batch, seq), heads(v, batch, seq), conv_p,
            mk.reshape(mem_shape), mv.reshape(mem_shape),
            heads(ks, dec_batch, dec_seq), heads(vs, dec_batch, dec_seq), conv_s)
```

```python
import functools

import jax
import jax.numpy as jnp
from jax import lax
from jax.experimental import pallas as pl
from jax.experimental.pallas import tpu as pltpu

F32 = jnp.float32
BF16 = jnp.bfloat16

D_MODEL = 1024
SB_HEAD_DIM = 64
SB_WIDTH = 512
SB_HEADS = SB_WIDTH // SB_HEAD_DIM
SB_SCALE = SB_HEAD_DIM ** -0.5
CONV_CH = 512
CONV_K = 3
N_MEM = 256
CA_HEADS = 4
CA_HEAD_DIM = 128
CA_WIDTH = CA_HEADS * CA_HEAD_DIM
CA_SCALE = CA_HEAD_DIM ** -0.5
PAGE_SIZE = 128
EPS = 1e-6
LOG2E = 1.4426950408889634

LANES = 128
SUBLANES = 8
MXU_DIM = 256
VMEM_LIMIT_BYTES = 56 * 1024 * 1024

ROW_TILE = 512
SB_BLOCK = 256
FFN_CHUNK = 256
PAGES_PER_STEP = 16


def _cparams(*sem):
    return pltpu.CompilerParams(dimension_semantics=sem, vmem_limit_bytes=VMEM_LIMIT_BYTES)


def _resident(shape):
    return pl.BlockSpec(shape, lambda *_: (0,) * len(shape), pipeline_mode=pl.Buffered(1))


def _rms(x, g):
    ms = jnp.mean(x * x, axis=-1, keepdims=True)
    return x * lax.rsqrt(ms + EPS) * g


def _dot(a, b):
    return jnp.dot(a, b, preferred_element_type=F32)


def _dot_nt(a, b):
    return lax.dot_general(a, b, (((1,), (1,)), ((), ())), preferred_element_type=F32)


def _log_sigmoids(z):
    soft = jnp.log(1.0 + jnp.exp2(jnp.abs(z) * -LOG2E))
    log_beta = jnp.minimum(z, 0.0) - soft
    return log_beta, log_beta - z


def _suffix_sums(log_om, tmat):
    return _dot(log_om.astype(BF16), tmat)


def _strict_lower_ones(n):
    row = lax.broadcasted_iota(jnp.int32, (n, n), 0)
    col = lax.broadcasted_iota(jnp.int32, (n, n), 1)
    return (row > col).astype(BF16)


def _in_proj_body(x_ref, g_ref, w_ref, wkv_ref, q_ref, kt_ref, vt_ref, ktb_ref, vtb_ref, bg_ref, u_ref):
    hn = _rms(x_ref[...], g_ref[...]).astype(BF16)

    def proj(c):
        return _dot(hn, w_ref[:, c * SB_WIDTH:(c + 1) * SB_WIDTH])

    q_ref[...] = (proj(0) * SB_SCALE).astype(BF16)
    bg_ref[...] = proj(1)
    u_ref[...] = proj(2) * proj(3)
    kv_t = _dot_nt(wkv_ref[...], hn)
    kt, vt = kv_t[:SB_WIDTH], kv_t[SB_WIDTH:]
    kt_ref[...] = kt
    vt_ref[...] = vt
    for c in range(ktb_ref.shape[0]):
        cols = slice(c * SB_BLOCK, (c + 1) * SB_BLOCK)
        ktb_ref[c] = kt[:, cols].astype(BF16)
        vtb_ref[c] = vt[:, cols].astype(BF16)


def _in_proj(x, g, w_rows, w_kv_t, batch, seq):
    n = batch * seq
    tm = min(ROW_TILE, seq)
    tiles = seq // tm
    blocks = tm // SB_BLOCK
    row = lambda width: pl.BlockSpec((tm, width), lambda b, i: (b * tiles + i, 0))
    t_spec = pl.BlockSpec((None, SB_WIDTH, tm), lambda b, i: (b, 0, i))
    tb_spec = pl.BlockSpec((None, blocks, SB_WIDTH, SB_BLOCK), lambda b, i: (b, i, 0, 0))
    rows = lambda dt: jax.ShapeDtypeStruct((n, SB_WIDTH), dt)
    t_shape = jax.ShapeDtypeStruct((batch, SB_WIDTH, seq), F32)
    tb_shape = jax.ShapeDtypeStruct((batch, seq // SB_BLOCK, SB_WIDTH, SB_BLOCK), BF16)
    return pl.pallas_call(
        _in_proj_body,
        grid=(batch, tiles),
        in_specs=[row(D_MODEL), _resident((1, D_MODEL)), _resident(w_rows.shape),
                  _resident(w_kv_t.shape)],
        out_specs=[row(SB_WIDTH), t_spec, t_spec, tb_spec, tb_spec, row(SB_WIDTH), row(SB_WIDTH)],
        out_shape=[rows(BF16), t_shape, t_shape, tb_shape, tb_shape, rows(F32), rows(F32)],
        compiler_params=_cparams("parallel", "parallel"),
        name="in_proj",
    )(x, g, w_rows, w_kv_t)


def _sb_prompt_body(bias_ref, q_ref, k_ref, v_ref, o_ref, qm_ref, acc_ref, carry_ref):
    blk = SB_BLOCK
    n_pairs = SB_HEADS // 2
    i = pl.program_id(1)
    row = lax.broadcasted_iota(jnp.int32, (2 * blk, blk), 0)
    col = lax.broadcasted_iota(jnp.int32, (2 * blk, blk), 1)
    readable = col < (row & (blk - 1))
    top = lax.broadcasted_iota(jnp.int32, (2 * blk, 1), 0) < blk
    tmat = _strict_lower_ones(blk)
    left = lax.broadcasted_iota(jnp.int32, (blk, LANES), 1) < SB_HEAD_DIM

    for pair in range(n_pairs):
        q2 = q_ref[:, pl.ds(pair * LANES, LANES)]
        zero = jnp.zeros_like(q2)
        qm_ref[pair] = jnp.concatenate(
            [jnp.where(left, q2, zero), jnp.where(left, zero, q2)], axis=0)

    def block(j, diagonal):
        kt_blk, vt_blk = k_ref[j], v_ref[j]
        carries = None if diagonal else [carry_ref[pair] for pair in range(n_pairs)]
        pairs = range(n_pairs)
        group = [slice(pair * LANES, (pair + 1) * LANES) for pair in pairs]
        qk = [_dot(qm_ref[pair], kt_blk[group[pair]]) for pair in pairs]
        log_beta, log_om = [], []
        for pair in pairs:
            bias = jnp.where(top, bias_ref[2 * pair], bias_ref[2 * pair + 1])
            lb, lo = _log_sigmoids(qk[pair] + bias)
            log_beta.append(lb)
            log_om.append(jnp.where(readable, lo, 0.0) if diagonal else lo)
        suffix = [_suffix_sums(log_om[pair], tmat) for pair in pairs]
        if diagonal:
            a = [jnp.where(readable, jnp.exp(log_beta[pair] + suffix[pair]), 0.0) for pair in pairs]
        else:
            a = [jnp.exp(log_beta[pair] + suffix[pair] + carries[pair]) for pair in pairs]
        outs = [_dot_nt(a[pair].astype(BF16), vt_blk[group[pair]]) for pair in pairs]
        totals = [jnp.sum(log_om[pair], axis=1, keepdims=True) for pair in pairs]
        for pair in pairs:
            if diagonal:
                acc_ref[pair] = outs[pair]
                carry_ref[pair] = totals[pair]
            else:
                acc_ref[pair] += outs[pair]
                carry_ref[pair] = carries[pair] + totals[pair]

    block(i, True)

    @pl.loop(0, i)
    def _(jj):
        block(i - 1 - jj, False)

    for pair in range(n_pairs):
        o = acc_ref[pair]
        o_ref[:, pl.ds(pair * LANES, LANES)] = jnp.where(left, o[:blk], o[blk:])


def _sb_prompt(q, ktb, vtb, bias, batch, seq):
    nq = seq // SB_BLOCK
    n_pairs = SB_HEADS // 2
    return pl.pallas_call(
        _sb_prompt_body,
        grid=(batch, nq),
        in_specs=[
            pl.BlockSpec(memory_space=pltpu.SMEM),
            pl.BlockSpec((SB_BLOCK, SB_WIDTH), lambda b, i: (b * nq + i, 0)),
            pl.BlockSpec((None, nq, SB_WIDTH, SB_BLOCK), lambda b, i: (b, 0, 0, 0)),
            pl.BlockSpec((None, nq, SB_WIDTH, SB_BLOCK), lambda b, i: (b, 0, 0, 0)),
        ],
        out_specs=pl.BlockSpec((SB_BLOCK, SB_WIDTH), lambda b, i: (b * nq + i, 0)),
        out_shape=jax.ShapeDtypeStruct((batch * seq, SB_WIDTH), F32),
        scratch_shapes=[pltpu.VMEM((n_pairs, 2 * SB_BLOCK, LANES), BF16),
                        pltpu.VMEM((n_pairs, 2 * SB_BLOCK, LANES), F32),
                        pltpu.VMEM((n_pairs, 2 * SB_BLOCK, 1), F32)],
        compiler_params=_cparams("parallel", "arbitrary"),
        name="sb_prompt",
    )(bias, q, ktb, vtb)


def _sb_sample_body(pt_ref, q_ref, kn_ref, vn_ref, brow_ref, *refs, t_new):
    del pt_ref
    pps = PAGES_PER_STEP
    k_refs, v_refs = refs[:pps], refs[pps:2 * pps]
    o_ref, qbd_ref, carry_ref, acc_ref = refs[2 * pps:]
    rows = SB_HEADS * t_new
    j = pl.program_id(1)
    tmat = _strict_lower_ones(PAGE_SIZE)
    row_head = lax.broadcasted_iota(jnp.int32, (rows, SB_WIDTH), 0) // t_new
    col_head = lax.broadcasted_iota(jnp.int32, (rows, SB_WIDTH), 1) // SB_HEAD_DIM

    def pages(k_pages, v_pages):
        kt = jnp.concatenate([k[...] for k in k_pages], axis=1).astype(BF16)
        vt = jnp.concatenate([v[...] for v in v_pages], axis=1).astype(BF16)
        log_beta, log_om = _log_sigmoids(_dot(qbd_ref[...], kt) + brow_ref[...])
        carry = carry_ref[...]
        a = []
        for s in range(len(k_pages)):
            cols = slice(s * PAGE_SIZE, (s + 1) * PAGE_SIZE)
            a.append(jnp.exp(log_beta[:, cols] + _suffix_sums(log_om[:, cols], tmat) + carry))
            carry = carry + jnp.sum(log_om[:, cols], axis=1, keepdims=True)
        carry_ref[...] = carry
        acc_ref[...] += _dot_nt(jnp.concatenate(a, axis=1).astype(BF16), vt)

    @pl.when(j == 0)
    def _():
        q_rows = jnp.concatenate([q_ref[...]] * SB_HEADS, axis=0)
        qbd = jnp.where(row_head == col_head, q_rows, 0.0).astype(BF16)
        qbd_ref[...] = qbd
        pad = jnp.zeros((PAGE_SIZE - t_new, SB_WIDTH), F32)
        k_new = jnp.concatenate([kn_ref[...], pad], axis=0).astype(BF16)
        v_new = jnp.concatenate([vn_ref[...], pad], axis=0).astype(BF16)
        key = lax.broadcasted_iota(jnp.int32, (rows, PAGE_SIZE), 1)
        readable = key < lax.broadcasted_iota(jnp.int32, (rows, PAGE_SIZE), 0) % t_new
        log_beta, log_om = _log_sigmoids(_dot_nt(qbd, k_new) + brow_ref[:, :PAGE_SIZE])
        log_om = jnp.where(readable, log_om, 0.0)
        a = jnp.where(readable, jnp.exp(log_beta + _suffix_sums(log_om, tmat)), 0.0)
        acc_ref[...] = _dot(a.astype(BF16), v_new)
        carry_ref[...] = jnp.broadcast_to(jnp.sum(log_om, axis=1, keepdims=True), carry_ref.shape)

    pages(k_refs, v_refs)

    @pl.when(j == pl.num_programs(1) - 1)
    def _():
        acc = acc_ref[...]
        o = jnp.zeros((t_new, SB_WIDTH), F32)
        for head in range(SB_HEADS):
            band = acc[head * t_new:(head + 1) * t_new]
            o = o + jnp.where(col_head[:t_new] == head, band, 0.0)
        o_ref[...] = o


def _sb_sample(q, k_new, v_new, cache_k, cache_v, page_table, bias, batch, t_new):
    n_pages = page_table.shape[1]
    pps = PAGES_PER_STEP
    rows = SB_HEADS * t_new
    brow = jnp.broadcast_to(jnp.repeat(bias, t_new)[:, None], (rows, pps * PAGE_SIZE))
    page_block = (None, SB_WIDTH, PAGE_SIZE)
    keys_minor = lambda c: c.transpose(0, 2, 3, 1).reshape(c.shape[0], SB_WIDTH, PAGE_SIZE)

    def page_spec(s):
        def index(b, j, pt):
            logical = n_pages - 1 - (j * pps + s)
            return (pt[b * n_pages + logical], 0, 0)
        return pl.BlockSpec(page_block, index)

    tok_spec = pl.BlockSpec((t_new, SB_WIDTH), lambda b, j, pt: (b, 0))
    grid_spec = pltpu.PrefetchScalarGridSpec(
        num_scalar_prefetch=1,
        grid=(batch, n_pages // pps),
        in_specs=[tok_spec, tok_spec, tok_spec, pl.BlockSpec(brow.shape, lambda b, j, pt: (0, 0))]
                 + [page_spec(s) for s in range(pps)] * 2,
        out_specs=tok_spec,
        scratch_shapes=[pltpu.VMEM((rows, SB_WIDTH), BF16),
                        pltpu.VMEM((rows, PAGE_SIZE), F32),
                        pltpu.VMEM((rows, SB_WIDTH), F32)],
    )
    return pl.pallas_call(
        functools.partial(_sb_sample_body, t_new=t_new),
        grid_spec=grid_spec,
        out_shape=jax.ShapeDtypeStruct((batch * t_new, SB_WIDTH), F32),
        compiler_params=_cparams("parallel", "arbitrary"),
        name="sb_sample",
    )(page_table.reshape(-1), q, k_new, v_new, brow,
      *([keys_minor(cache_k)] * pps), *([keys_minor(cache_v)] * pps))


def _mixer_out_body(x_ref, a_ref, bg_ref, u_ref, uprev_ref, init_ref, wc_ref, ga_ref, gc_ref,
                    wo_ref, h_ref, ext_ref, *, tiles_per_seq):
    tm = u_ref.shape[0]
    i = pl.program_id(0)
    if tiles_per_seq == 1:
        halo = init_ref[0]
    else:
        halo = jnp.where(lax.rem(i, tiles_per_seq) == 0, init_ref[0], uprev_ref[...])
    u = u_ref[...]
    ext_ref[0:SUBLANES, :] = halo
    ext_ref[SUBLANES:SUBLANES + tm, :] = u
    u1 = ext_ref[SUBLANES - 1:SUBLANES - 1 + tm, :]
    u2 = ext_ref[SUBLANES - 2:SUBLANES - 2 + tm, :]
    wc = wc_ref[...]
    conv = wc[0:1] * u2 + wc[1:2] * u1 + wc[2:3] * u
    c = _rms(bg_ref[...] * conv, gc_ref[...]).astype(BF16)
    a = _rms(a_ref[...], ga_ref[...]).astype(BF16)
    h_ref[...] = (x_ref[...] + _dot(a, wo_ref[0:SB_WIDTH, :])
                  + _dot(c, wo_ref[SB_WIDTH:SB_WIDTH + CONV_CH, :]))


def _mixer_out(x, attn, bg, u, init, w_conv, g_attn, g_conv, w_out, seq):
    n = x.shape[0]
    tm = min(ROW_TILE, seq)
    tiles_per_seq = seq // tm
    per8 = tm // SUBLANES
    row = lambda width: pl.BlockSpec((tm, width), lambda i: (i, 0))
    return pl.pallas_call(
        functools.partial(_mixer_out_body, tiles_per_seq=tiles_per_seq),
        grid=(n // tm,),
        in_specs=[
            row(D_MODEL), row(SB_WIDTH), row(CONV_CH), row(CONV_CH),
            pl.BlockSpec((SUBLANES, CONV_CH), lambda i: (jnp.maximum(i * per8 - 1, 0), 0)),
            pl.BlockSpec((1, SUBLANES, CONV_CH), lambda i: (i // tiles_per_seq, 0, 0)),
            _resident((CONV_K, CONV_CH)), _resident((1, SB_WIDTH)), _resident((1, CONV_CH)),
            _resident(w_out.shape),
        ],
        out_specs=row(D_MODEL),
        out_shape=jax.ShapeDtypeStruct((n, D_MODEL), F32),
        scratch_shapes=[pltpu.VMEM((tm + SUBLANES, CONV_CH), F32)],
        compiler_params=_cparams("parallel"),
        name="mixer_out",
    )(x, attn, bg, u, u, init, w_conv, g_attn, g_conv, w_out)


def _memory_kv_body(m_ref, g_ref, wk_ref, wv_ref, k_ref, v_ref):
    mn = _rms(m_ref[...], g_ref[...]).astype(BF16)
    k_ref[...] = _dot(mn, wk_ref[...])
    v_ref[...] = _dot(mn, wv_ref[...])


def _memory_kv(mem, g, w_ck, w_cv):
    n = mem.shape[0]
    tm = min(ROW_TILE, n)
    row = lambda width: pl.BlockSpec((tm, width), lambda i: (i, 0))
    return pl.pallas_call(
        _memory_kv_body,
        grid=(n // tm,),
        in_specs=[row(D_MODEL), _resident((1, D_MODEL)), _resident(w_ck.shape), _resident(w_cv.shape)],
        out_specs=[row(CA_WIDTH)] * 2,
        out_shape=[jax.ShapeDtypeStruct((n, CA_WIDTH), F32)] * 2,
        compiler_params=_cparams("parallel"),
        name="memory_kv",
    )(mem, g, w_ck, w_cv)


def _cross_body(h_ref, g_ref, wq_ref, mk_ref, mv_ref, wo_ref, o_ref):
    h = h_ref[...]
    q = _dot(_rms(h, g_ref[...]).astype(BF16), wq_ref[...])
    outs = []
    for head in range(CA_HEADS):
        lanes = pl.ds(head * CA_HEAD_DIM, CA_HEAD_DIM)
        qh = q[:, head * CA_HEAD_DIM:(head + 1) * CA_HEAD_DIM].astype(BF16)
        sc = _dot_nt(qh, mk_ref[:, lanes].astype(BF16)) * CA_SCALE
        p = jnp.exp(sc - jnp.max(sc, axis=-1, keepdims=True))
        denom = jnp.sum(p, axis=-1, keepdims=True)
        outs.append(_dot(p.astype(BF16), mv_ref[:, lanes].astype(BF16)) / denom)
    o = jnp.concatenate(outs, axis=-1).astype(BF16)
    o_ref[...] = h + _dot(o, wo_ref[...])


def _cross(h, g, w_cq, mk, mv, w_co, rows_per_batch):
    n = h.shape[0]
    tm = min(ROW_TILE, rows_per_batch)
    tiles_per_batch = rows_per_batch // tm
    row = pl.BlockSpec((tm, D_MODEL), lambda i: (i, 0))
    mem = pl.BlockSpec((N_MEM, CA_WIDTH), lambda i: (i // tiles_per_batch, 0))
    return pl.pallas_call(
        _cross_body,
        grid=(n // tm,),
        in_specs=[row, _resident((1, D_MODEL)), _resident(w_cq.shape), mem, mem,
                  _resident(w_co.shape)],
        out_specs=row,
        out_shape=jax.ShapeDtypeStruct((n, D_MODEL), F32),
        compiler_params=_cparams("parallel"),
        name="cross_attn",
    )(h, g, w_cq, mk, mv, w_co)


def _ffn_body(h_ref, g_ref, wg_ref, wu_ref, wd_ref, gf_ref, y_ref):
    h = h_ref[...]
    hn = _rms(h, g_ref[...]).astype(BF16)
    acc = h
    for c in range(0, wg_ref.shape[1], FFN_CHUNK):
        gate = _dot(hn, wg_ref[:, c:c + FFN_CHUNK])
        up = _dot(hn, wu_ref[:, c:c + FFN_CHUNK])
        act = (gate / (1.0 + jnp.exp(-gate)) * up).astype(BF16)
        acc = acc + _dot(act, wd_ref[c:c + FFN_CHUNK, :])
    y_ref[...] = _rms(acc, gf_ref[...])


def _ffn(h, g, w_gate, w_up, w_down, g_final):
    n = h.shape[0]
    tm = min(ROW_TILE, n)
    row = pl.BlockSpec((tm, D_MODEL), lambda i: (i, 0))
    return pl.pallas_call(
        _ffn_body,
        grid=(n // tm,),
        in_specs=[row, _resident((1, D_MODEL)), _resident(w_gate.shape), _resident(w_up.shape),
                  _resident(w_down.shape), _resident((1, D_MODEL))],
        out_specs=row,
        out_shape=jax.ShapeDtypeStruct((n, D_MODEL), F32),
        compiler_params=_cparams("parallel"),
        name="ffn",
    )(h, g, w_gate, w_up, w_down, g_final)


def kernel(x_prompt, x_sample, mem_prompt, cache_sb_k, cache_sb_v, cache_mem_k, cache_mem_v,
           state_conv, page_table, g_mix, w_in, sb_bias, w_conv, g_attn_out, g_conv_out, w_out,
           g_mem, g_ca, w_cq, w_ck, w_cv, w_co, g_ffn, w_gate, w_up, w_down, g_final):
    depth = w_in.shape[0]
    assert depth == 1, "single-layer stack"
    batch, seq, _ = x_prompt.shape
    dec_batch, dec_seq, _ = x_sample.shape
    n_s = dec_batch * dec_seq
    assert seq % ROW_TILE == 0 and ROW_TILE % SB_BLOCK == 0 and n_s % SB_BLOCK == 0
    assert dec_seq == SUBLANES
    l = 0
    bf = lambda w: w[l].astype(BF16)
    vec = lambda g: g[l][None, :]
    w_out_b = bf(w_out)
    w_in_b = bf(w_in)
    w_rows = jnp.concatenate([w_in_b[:, :SB_WIDTH], w_in_b[:, 3 * SB_WIDTH:]], axis=1)
    w_kv_t = w_in_b[:, SB_WIDTH:3 * SB_WIDTH].T
    w_cq_b, w_ck_b, w_cv_b, w_co_b = bf(w_cq), bf(w_ck), bf(w_cv), bf(w_co)
    w_gate_b, w_up_b, w_down_b = bf(w_gate), bf(w_up), bf(w_down)
    g_fin = g_final[None, :]
    heads = lambda t: t.reshape(1, t.shape[0], SB_HEADS, SB_HEAD_DIM, t.shape[2]).transpose(0, 1, 4, 2, 3)

    def tail(x2, attn, bg, u, init, mk, mv, seq_len, rows_per_batch):
        h = _mixer_out(x2, attn, bg, u, init, w_conv[l], vec(g_attn_out), vec(g_conv_out),
                       w_out_b, seq_len)
        h = _cross(h, vec(g_ca), w_cq_b, mk, mv, w_co_b, rows_per_batch)
        return _ffn(h, vec(g_ffn), w_gate_b, w_up_b, w_down_b, g_fin)

    xp = x_prompt.reshape(batch * seq, D_MODEL)
    q, kt, vt, ktb, vtb, bg, u = _in_proj(xp, vec(g_mix), w_rows, w_kv_t, batch, seq)
    attn = _sb_prompt(q, ktb, vtb, sb_bias[l], batch, seq)
    mk, mv = _memory_kv(mem_prompt.reshape(batch * N_MEM, D_MODEL), vec(g_mem), w_ck_b, w_cv_b)
    init_p = jnp.zeros((batch, SUBLANES, CONV_CH), F32)
    y_prompt = tail(xp, attn, bg, u, init_p, mk, mv, seq, seq).reshape(batch, seq, D_MODEL)
    conv_p = u.reshape(batch, seq, CONV_CH)[:, seq - (CONV_K - 1):][None]
    mem_shape = (1, batch, N_MEM, CA_HEADS, CA_HEAD_DIM)

    xs = x_sample.reshape(n_s, D_MODEL)
    qs, kts, vts, _, _, bgs, us = _in_proj(xs, vec(g_mix), w_rows, w_kv_t, 1, n_s)
    ks, vs = kts[0].T, vts[0].T
    attn_s = _sb_sample(qs.astype(F32), ks, vs, cache_sb_k[l], cache_sb_v[l],
                        page_table, sb_bias[l], dec_batch, dec_seq)
    init_s = jnp.pad(state_conv[l], ((0, 0), (SUBLANES - (CONV_K - 1), 0), (0, 0)))
    y_sample = tail(xs, attn_s, bgs, us, init_s,
                    cache_mem_k[l].reshape(dec_batch * N_MEM, CA_WIDTH),
                    cache_mem_v[l].reshape(dec_batch * N_MEM, CA_WIDTH),
                    dec_seq, dec_seq).reshape(dec_batch, dec_seq, D_MODEL)
    conv_s = us.reshape(dec_batch, dec_seq, CONV_CH)[:, dec_seq - (CONV_K - 1):][None]

    new_kv = lambda t: t.reshape(1, dec_batch, dec_seq, SB_HEADS, SB_HEAD_DIM)
    return (y_prompt, y_sample, heads(kt), heads(vt), conv_p,
            mk.reshape(mem_shape), mv.reshape(mem_shape), new_kv(ks), new_kv(vs), conv_s)
```

```python
import functools

import jax
import jax.numpy as jnp
from jax import lax
from jax.experimental import pallas as pl
from jax.experimental.pallas import tpu as pltpu

F32 = jnp.float32
BF16 = jnp.bfloat16

D_MODEL = 1024
SB_HEAD_DIM = 64
SB_WIDTH = 512
SB_HEADS = SB_WIDTH // SB_HEAD_DIM
SB_SCALE = SB_HEAD_DIM ** -0.5
CONV_CH = 512
CONV_K = 3
N_MEM = 256
CA_HEADS = 4
CA_HEAD_DIM = 128
CA_WIDTH = CA_HEADS * CA_HEAD_DIM
CA_SCALE = CA_HEAD_DIM ** -0.5
PAGE_SIZE = 128
EPS = 1e-6
LOG2E = 1.4426950408889634

LANES = 128
SUBLANES = 8
MXU_DIM = 256
VMEM_LIMIT_BYTES = 56 * 1024 * 1024

ROW_TILE = 512
SB_BLOCK = 256
FFN_CHUNK = 256
PAGES_PER_STEP = 16
CROSS_BATCHES_PER_TILE = 8

SB_STAGE_ORDER = (
    ("logits", (0, 1)), ("log_terms", (0, 1)), ("logits", (2, 3)), ("suffixes", (0, 1)),
    ("log_terms", (2, 3)), ("weights", (0, 1)), ("suffixes", (2, 3)),
    ("values", (0, 1)), ("weights", (2, 3)), ("values", (2, 3)),
)


def _cparams(*sem):
    return pltpu.CompilerParams(dimension_semantics=sem, vmem_limit_bytes=VMEM_LIMIT_BYTES)


def _resident(shape):
    return pl.BlockSpec(shape, lambda *_: (0,) * len(shape), pipeline_mode=pl.Buffered(1))


def _rms(x, g):
    ms = jnp.mean(x * x, axis=-1, keepdims=True)
    return x * lax.rsqrt(ms + EPS) * g


def _dot(a, b):
    return jnp.dot(a, b, preferred_element_type=F32)


def _dot_nt(a, b):
    return lax.dot_general(a, b, (((1,), (1,)), ((), ())), preferred_element_type=F32)


def _log_sigmoids(z):
    soft = jnp.log(1.0 + jnp.exp2(jnp.abs(z) * -LOG2E))
    log_beta = jnp.minimum(z, 0.0) - soft
    return log_beta, log_beta - z


def _suffix_sums(log_om, tmat):
    return _dot(log_om.astype(BF16), tmat)


def _strict_lower_ones(n):
    row = lax.broadcasted_iota(jnp.int32, (n, n), 0)
    col = lax.broadcasted_iota(jnp.int32, (n, n), 1)
    return (row > col).astype(BF16)


def _in_proj_body(x_ref, g_ref, w_ref, wkv_ref, q_ref, kt_ref, vt_ref, ktb_ref, vtb_ref, bg_ref, u_ref):
    hn = _rms(x_ref[...], g_ref[...]).astype(BF16)

    def proj(c):
        return _dot(hn, w_ref[:, c * SB_WIDTH:(c + 1) * SB_WIDTH])

    q_ref[...] = (proj(0) * SB_SCALE).astype(BF16)
    bg_ref[...] = proj(1)
    u_ref[...] = proj(2) * proj(3)
    kv_t = _dot_nt(wkv_ref[...], hn)
    kt, vt = kv_t[:SB_WIDTH], kv_t[SB_WIDTH:]
    kt_ref[...] = kt
    vt_ref[...] = vt
    for c in range(ktb_ref.shape[0]):
        cols = slice(c * SB_BLOCK, (c + 1) * SB_BLOCK)
        ktb_ref[c] = kt[:, cols].astype(BF16)
        vtb_ref[c] = vt[:, cols].astype(BF16)


def _in_proj(x, g, w_rows, w_kv_t, batch, seq):
    n = batch * seq
    tm = min(ROW_TILE, seq)
    tiles = seq // tm
    blocks = tm // SB_BLOCK
    row = lambda width: pl.BlockSpec((tm, width), lambda b, i: (b * tiles + i, 0))
    t_spec = pl.BlockSpec((None, SB_WIDTH, tm), lambda b, i: (b, 0, i))
    tb_spec = pl.BlockSpec((None, blocks, SB_WIDTH, SB_BLOCK), lambda b, i: (b, i, 0, 0))
    rows = lambda dt: jax.ShapeDtypeStruct((n, SB_WIDTH), dt)
    t_shape = jax.ShapeDtypeStruct((batch, SB_WIDTH, seq), F32)
    tb_shape = jax.ShapeDtypeStruct((batch, seq // SB_BLOCK, SB_WIDTH, SB_BLOCK), BF16)
    return pl.pallas_call(
        _in_proj_body,
        grid=(batch, tiles),
        in_specs=[row(D_MODEL), _resident((1, D_MODEL)), _resident(w_rows.shape),
                  _resident(w_kv_t.shape)],
        out_specs=[row(SB_WIDTH), t_spec, t_spec, tb_spec, tb_spec, row(SB_WIDTH), row(SB_WIDTH)],
        out_shape=[rows(BF16), t_shape, t_shape, tb_shape, tb_shape, rows(F32), rows(F32)],
        compiler_params=_cparams("parallel", "parallel"),
        name="in_proj",
    )(x, g, w_rows, w_kv_t)


def _sb_prompt_body(bias_ref, q_ref, k_ref, v_ref, o_ref, qm_ref, acc_ref, carry_ref):
    blk = SB_BLOCK
    n_pairs = SB_HEADS // 2
    i = pl.program_id(1)
    row = lax.broadcasted_iota(jnp.int32, (2 * blk, blk), 0)
    col = lax.broadcasted_iota(jnp.int32, (2 * blk, blk), 1)
    readable = col < (row & (blk - 1))
    top = lax.broadcasted_iota(jnp.int32, (2 * blk, 1), 0) < blk
    tmat = _strict_lower_ones(blk)
    left = lax.broadcasted_iota(jnp.int32, (blk, LANES), 1) < SB_HEAD_DIM

    for pair in range(n_pairs):
        q2 = q_ref[:, pl.ds(pair * LANES, LANES)]
        zero = jnp.zeros_like(q2)
        qm_ref[pair] = jnp.concatenate(
            [jnp.where(left, q2, zero), jnp.where(left, zero, q2)], axis=0)

    def block(j, diagonal):
        kt_blk, vt_blk = k_ref[j], v_ref[j]
        carries = None if diagonal else [carry_ref[pair] for pair in range(n_pairs)]
        pairs = range(n_pairs)
        group = [slice(pair * LANES, (pair + 1) * LANES) for pair in pairs]
        qk, log_beta, log_om, suffix, a, outs, totals = ({} for _ in range(7))

        def logits(pair):
            qk[pair] = _dot(qm_ref[pair], kt_blk[group[pair]])

        def log_terms(pair):
            bias = jnp.where(top, bias_ref[2 * pair], bias_ref[2 * pair + 1])
            log_beta[pair], lo = _log_sigmoids(qk[pair] + bias)
            log_om[pair] = jnp.where(readable, lo, 0.0) if diagonal else lo
            totals[pair] = jnp.sum(log_om[pair], axis=1, keepdims=True)

        def suffixes(pair):
            suffix[pair] = _suffix_sums(log_om[pair], tmat)

        def weights(pair):
            if diagonal:
                a[pair] = jnp.where(readable, jnp.exp(log_beta[pair] + suffix[pair]), 0.0)
            else:
                a[pair] = jnp.exp(log_beta[pair] + suffix[pair] + carries[pair])

        def values(pair):
            outs[pair] = _dot_nt(a[pair].astype(BF16), vt_blk[group[pair]])

        for stage, members in SB_STAGE_ORDER:
            for pair in members:
                {"logits": logits, "log_terms": log_terms, "suffixes": suffixes,
                 "weights": weights, "values": values}[stage](pair)
        for pair in pairs:
            if diagonal:
                acc_ref[pair] = outs[pair]
                carry_ref[pair] = totals[pair]
            else:
                acc_ref[pair] += outs[pair]
                carry_ref[pair] = carries[pair] + totals[pair]

    block(i, True)

    @pl.loop(0, i)
    def _(jj):
        block(i - 1 - jj, False)

    for pair in range(n_pairs):
        o = acc_ref[pair]
        o_ref[:, pl.ds(pair * LANES, LANES)] = jnp.where(left, o[:blk], o[blk:])


def _sb_prompt(q, ktb, vtb, bias, batch, seq):
    nq = seq // SB_BLOCK
    n_pairs = SB_HEADS // 2
    return pl.pallas_call(
        _sb_prompt_body,
        grid=(batch, nq),
        in_specs=[
            pl.BlockSpec(memory_space=pltpu.SMEM),
            pl.BlockSpec((SB_BLOCK, SB_WIDTH), lambda b, i: (b * nq + i, 0)),
            pl.BlockSpec((None, nq, SB_WIDTH, SB_BLOCK), lambda b, i: (b, 0, 0, 0)),
            pl.BlockSpec((None, nq, SB_WIDTH, SB_BLOCK), lambda b, i: (b, 0, 0, 0)),
        ],
        out_specs=pl.BlockSpec((SB_BLOCK, SB_WIDTH), lambda b, i: (b * nq + i, 0)),
        out_shape=jax.ShapeDtypeStruct((batch * seq, SB_WIDTH), F32),
        scratch_shapes=[pltpu.VMEM((n_pairs, 2 * SB_BLOCK, LANES), BF16),
                        pltpu.VMEM((n_pairs, 2 * SB_BLOCK, LANES), F32),
                        pltpu.VMEM((n_pairs, 2 * SB_BLOCK, 1), F32)],
        compiler_params=_cparams("parallel", "arbitrary"),
        name="sb_prompt",
    )(bias, q, ktb, vtb)


def _sb_sample_body(pt_ref, q_ref, kn_ref, vn_ref, brow_ref, *refs, t_new):
    del pt_ref
    pps = PAGES_PER_STEP
    k_refs, v_refs = refs[:pps], refs[pps:2 * pps]
    o_ref, qbd_ref, carry_ref, acc_ref = refs[2 * pps:]
    rows = SB_HEADS * t_new
    j = pl.program_id(1)
    tmat = _strict_lower_ones(PAGE_SIZE)
    row_head = lax.broadcasted_iota(jnp.int32, (rows, SB_WIDTH), 0) // t_new
    col_head = lax.broadcasted_iota(jnp.int32, (rows, SB_WIDTH), 1) // SB_HEAD_DIM

    def pages(k_pages, v_pages):
        kt = jnp.concatenate([k[...] for k in k_pages], axis=1).astype(BF16)
        vt = jnp.concatenate([v[...] for v in v_pages], axis=1).astype(BF16)
        log_beta, log_om = _log_sigmoids(_dot(qbd_ref[...], kt) + brow_ref[...])
        carry = carry_ref[...]
        a = []
        for s in range(len(k_pages)):
            cols = slice(s * PAGE_SIZE, (s + 1) * PAGE_SIZE)
            a.append(jnp.exp(log_beta[:, cols] + _suffix_sums(log_om[:, cols], tmat) + carry))
            carry = carry + jnp.sum(log_om[:, cols], axis=1, keepdims=True)
        carry_ref[...] = carry
        acc_ref[...] += _dot_nt(jnp.concatenate(a, axis=1).astype(BF16), vt)

    @pl.when(j == 0)
    def _():
        q_rows = jnp.concatenate([q_ref[...]] * SB_HEADS, axis=0)
        qbd = jnp.where(row_head == col_head, q_rows, 0.0).astype(BF16)
        qbd_ref[...] = qbd
        pad = jnp.zeros((PAGE_SIZE - t_new, SB_WIDTH), F32)
        k_new = jnp.concatenate([kn_ref[...], pad], axis=0).astype(BF16)
        v_new = jnp.concatenate([vn_ref[...], pad], axis=0).astype(BF16)
        key = lax.broadcasted_iota(jnp.int32, (rows, PAGE_SIZE), 1)
        readable = key < lax.broadcasted_iota(jnp.int32, (rows, PAGE_SIZE), 0) % t_new
        log_beta, log_om = _log_sigmoids(_dot_nt(qbd, k_new) + brow_ref[:, :PAGE_SIZE])
        log_om = jnp.where(readable, log_om, 0.0)
        a = jnp.where(readable, jnp.exp(log_beta + _suffix_sums(log_om, tmat)), 0.0)
        acc_ref[...] = _dot(a.astype(BF16), v_new)
        carry_ref[...] = jnp.broadcast_to(jnp.sum(log_om, axis=1, keepdims=True), carry_ref.shape)

    pages(k_refs, v_refs)

    @pl.when(j == pl.num_programs(1) - 1)
    def _():
        acc = acc_ref[...]
        o = jnp.zeros((t_new, SB_WIDTH), F32)
        for head in range(SB_HEADS):
            band = acc[head * t_new:(head + 1) * t_new]
            o = o + jnp.where(col_head[:t_new] == head, band, 0.0)
        o_ref[...] = o


def _sb_sample(q, k_new, v_new, cache_k, cache_v, page_table, bias, batch, t_new):
    n_pages = page_table.shape[1]
    pps = PAGES_PER_STEP
    rows = SB_HEADS * t_new
    brow = jnp.broadcast_to(jnp.repeat(bias, t_new)[:, None], (rows, pps * PAGE_SIZE))
    page_block = (None, SB_WIDTH, PAGE_SIZE)
    keys_minor = lambda c: c.transpose(0, 2, 3, 1).reshape(c.shape[0], SB_WIDTH, PAGE_SIZE)

    def page_spec(s):
        def index(b, j, pt):
            logical = n_pages - 1 - (j * pps + s)
            return (pt[b * n_pages + logical], 0, 0)
        return pl.BlockSpec(page_block, index)

    tok_spec = pl.BlockSpec((t_new, SB_WIDTH), lambda b, j, pt: (b, 0))
    grid_spec = pltpu.PrefetchScalarGridSpec(
        num_scalar_prefetch=1,
        grid=(batch, n_pages // pps),
        in_specs=[tok_spec, tok_spec, tok_spec, pl.BlockSpec(brow.shape, lambda b, j, pt: (0, 0))]
                 + [page_spec(s) for s in range(pps)] * 2,
        out_specs=tok_spec,
        scratch_shapes=[pltpu.VMEM((rows, SB_WIDTH), BF16),
                        pltpu.VMEM((rows, PAGE_SIZE), F32),
                        pltpu.VMEM((rows, SB_WIDTH), F32)],
    )
    return pl.pallas_call(
        functools.partial(_sb_sample_body, t_new=t_new),
        grid_spec=grid_spec,
        out_shape=jax.ShapeDtypeStruct((batch * t_new, SB_WIDTH), F32),
        compiler_params=_cparams("parallel", "arbitrary"),
        name="sb_sample",
    )(page_table.reshape(-1), q, k_new, v_new, brow,
      *([keys_minor(cache_k)] * pps), *([keys_minor(cache_v)] * pps))


def _mixer_math(x_ref, a_ref, bg_ref, u_ref, uprev_ref, init_ref, wc_ref, ga_ref, gc_ref,
                wo_ref, ext_ref, tiles_per_seq):
    tm = u_ref.shape[0]
    i = pl.program_id(0)
    if tiles_per_seq == 1:
        halo = init_ref[0]
    else:
        halo = jnp.where(lax.rem(i, tiles_per_seq) == 0, init_ref[0], uprev_ref[...])
    u = u_ref[...]
    ext_ref[0:SUBLANES, :] = halo
    ext_ref[SUBLANES:SUBLANES + tm, :] = u
    u1 = ext_ref[SUBLANES - 1:SUBLANES - 1 + tm, :]
    u2 = ext_ref[SUBLANES - 2:SUBLANES - 2 + tm, :]
    wc = wc_ref[...]
    conv = wc[0:1] * u2 + wc[1:2] * u1 + wc[2:3] * u
    c = _rms(bg_ref[...] * conv, gc_ref[...]).astype(BF16)
    a = _rms(a_ref[...], ga_ref[...]).astype(BF16)
    return (x_ref[...] + _dot(a, wo_ref[0:SB_WIDTH, :])
            + _dot(c, wo_ref[SB_WIDTH:SB_WIDTH + CONV_CH, :]))


def _mixer_out_body(*refs, tiles_per_seq):
    *ins, h_ref, ext_ref = refs
    h_ref[...] = _mixer_math(*ins, ext_ref, tiles_per_seq)


def _mixer_specs(tm, tiles_per_seq, w_out_shape):
    per8 = tm // SUBLANES
    row = lambda width: pl.BlockSpec((tm, width), lambda i: (i, 0))
    return [
        row(D_MODEL), row(SB_WIDTH), row(CONV_CH), row(CONV_CH),
        pl.BlockSpec((SUBLANES, CONV_CH), lambda i: (jnp.maximum(i * per8 - 1, 0), 0)),
        pl.BlockSpec((1, SUBLANES, CONV_CH), lambda i: (i // tiles_per_seq, 0, 0)),
        _resident((CONV_K, CONV_CH)), _resident((1, SB_WIDTH)), _resident((1, CONV_CH)),
        _resident(w_out_shape),
    ]


def _mixer_out(x, attn, bg, u, init, w_conv, g_attn, g_conv, w_out, seq):
    n = x.shape[0]
    tm = min(ROW_TILE, seq)
    tiles_per_seq = seq // tm
    return pl.pallas_call(
        functools.partial(_mixer_out_body, tiles_per_seq=tiles_per_seq),
        grid=(n // tm,),
        in_specs=_mixer_specs(tm, tiles_per_seq, w_out.shape),
        out_specs=pl.BlockSpec((tm, D_MODEL), lambda i: (i, 0)),
        out_shape=jax.ShapeDtypeStruct((n, D_MODEL), F32),
        scratch_shapes=[pltpu.VMEM((tm + SUBLANES, CONV_CH), F32)],
        compiler_params=_cparams("parallel"),
        name="mixer_out",
    )(x, attn, bg, u, u, init, w_conv, g_attn, g_conv, w_out)


def _memory_kv_body(m_ref, g_ref, wk_ref, wv_ref, k_ref, v_ref):
    mn = _rms(m_ref[...], g_ref[...]).astype(BF16)
    k_ref[...] = _dot(mn, wk_ref[...])
    v_ref[...] = _dot(mn, wv_ref[...])


def _memory_kv(mem, g, w_ck, w_cv):
    n = mem.shape[0]
    tm = min(ROW_TILE, n)
    row = lambda width: pl.BlockSpec((tm, width), lambda i: (i, 0))
    return pl.pallas_call(
        _memory_kv_body,
        grid=(n // tm,),
        in_specs=[row(D_MODEL), _resident((1, D_MODEL)), _resident(w_ck.shape), _resident(w_cv.shape)],
        out_specs=[row(CA_WIDTH)] * 2,
        out_shape=[jax.ShapeDtypeStruct((n, CA_WIDTH), F32)] * 2,
        compiler_params=_cparams("parallel"),
        name="memory_kv",
    )(mem, g, w_ck, w_cv)


def _cross_math(h, g_ref, wq_ref, mk_ref, mv_ref, wo_ref):
    n_b = mk_ref.shape[0] // N_MEM
    r = h.shape[0] // n_b
    q = _dot(_rms(h, g_ref[...]).astype(BF16), wq_ref[...])
    per_batch = []
    for b in range(n_b):
        mem_rows = pl.ds(b * N_MEM, N_MEM)
        outs = []
        for head in range(CA_HEADS):
            lanes = pl.ds(head * CA_HEAD_DIM, CA_HEAD_DIM)
            qh = q[b * r:(b + 1) * r, head * CA_HEAD_DIM:(head + 1) * CA_HEAD_DIM].astype(BF16)
            sc = _dot_nt(qh, mk_ref[mem_rows, lanes].astype(BF16)) * CA_SCALE
            p = jnp.exp(sc - jnp.max(sc, axis=-1, keepdims=True))
            denom = jnp.sum(p, axis=-1, keepdims=True)
            outs.append(_dot(p.astype(BF16), mv_ref[mem_rows, lanes].astype(BF16)) / denom)
        per_batch.append(jnp.concatenate(outs, axis=-1))
    o = jnp.concatenate(per_batch, axis=0).astype(BF16)
    return h + _dot(o, wo_ref[...])


def _cross_body(h_ref, *refs):
    *ins, o_ref = refs
    o_ref[...] = _cross_math(h_ref[...], *ins)


def _cross(h, g, w_cq, mk, mv, w_co, rows_per_batch):
    n = h.shape[0]
    assert rows_per_batch < ROW_TILE
    n_b = min(CROSS_BATCHES_PER_TILE, n // rows_per_batch)
    tm = n_b * rows_per_batch
    row = pl.BlockSpec((tm, D_MODEL), lambda i: (i, 0))
    mem = pl.BlockSpec((n_b * N_MEM, CA_WIDTH), lambda i: (i, 0))
    return pl.pallas_call(
        _cross_body,
        grid=(n // tm,),
        in_specs=[row, _resident((1, D_MODEL)), _resident(w_cq.shape), mem, mem,
                  _resident(w_co.shape)],
        out_specs=row,
        out_shape=jax.ShapeDtypeStruct((n, D_MODEL), F32),
        compiler_params=_cparams("parallel"),
        name="cross_attn",
    )(h, g, w_cq, mk, mv, w_co)


def _ffn_math(h, g_ref, wg_ref, wu_ref, wd_ref, gf_ref):
    hn = _rms(h, g_ref[...]).astype(BF16)
    acc = h
    for c in range(0, wg_ref.shape[1], FFN_CHUNK):
        gate = _dot(hn, wg_ref[:, c:c + FFN_CHUNK])
        up = _dot(hn, wu_ref[:, c:c + FFN_CHUNK])
        act = (gate / (1.0 + jnp.exp(-gate)) * up).astype(BF16)
        acc = acc + _dot(act, wd_ref[c:c + FFN_CHUNK, :])
    return _rms(acc, gf_ref[...])


def _ffn_body(h_ref, *refs):
    *ins, y_ref = refs
    y_ref[...] = _ffn_math(h_ref[...], *ins)


def _ffn(h, g, w_gate, w_up, w_down, g_final):
    n = h.shape[0]
    tm = min(ROW_TILE, n)
    row = pl.BlockSpec((tm, D_MODEL), lambda i: (i, 0))
    return pl.pallas_call(
        _ffn_body,
        grid=(n // tm,),
        in_specs=[row, _resident((1, D_MODEL)), _resident(w_gate.shape), _resident(w_up.shape),
                  _resident(w_down.shape), _resident((1, D_MODEL))],
        out_specs=row,
        out_shape=jax.ShapeDtypeStruct((n, D_MODEL), F32),
        compiler_params=_cparams("parallel"),
        name="ffn",
    )(h, g, w_gate, w_up, w_down, g_final)


N_MIXER_INS, N_CROSS_INS = 10, 5


def _tail_body(*refs, tiles_per_seq):
    *ins, y_ref, ext_ref = refs
    mixer_ins, ins = ins[:N_MIXER_INS], ins[N_MIXER_INS:]
    cross_ins, ffn_ins = ins[:N_CROSS_INS], ins[N_CROSS_INS:]
    h = _mixer_math(*mixer_ins, ext_ref, tiles_per_seq)
    h = _cross_math(h, *cross_ins)
    y_ref[...] = _ffn_math(h, *ffn_ins)


def _tail(x, attn, bg, u, init, w_conv, g_attn, g_conv, w_out, g_ca, w_cq, mk, mv, w_co,
          g_ffn, w_gate, w_up, w_down, g_final, seq):
    n = x.shape[0]
    tm = min(ROW_TILE, seq)
    tiles_per_seq = seq // tm
    vec_spec = _resident((1, D_MODEL))
    mem = pl.BlockSpec((N_MEM, CA_WIDTH), lambda i: (i // tiles_per_seq, 0))
    return pl.pallas_call(
        functools.partial(_tail_body, tiles_per_seq=tiles_per_seq),
        grid=(n // tm,),
        in_specs=_mixer_specs(tm, tiles_per_seq, w_out.shape)
                 + [vec_spec, _resident(w_cq.shape), mem, mem, _resident(w_co.shape)]
                 + [vec_spec, _resident(w_gate.shape), _resident(w_up.shape),
                    _resident(w_down.shape), vec_spec],
        out_specs=pl.BlockSpec((tm, D_MODEL), lambda i: (i, 0)),
        out_shape=jax.ShapeDtypeStruct((n, D_MODEL), F32),
        scratch_shapes=[pltpu.VMEM((tm + SUBLANES, CONV_CH), F32)],
        compiler_params=_cparams("parallel"),
        name="tail",
    )(x, attn, bg, u, u, init, w_conv, g_attn, g_conv, w_out, g_ca, w_cq, mk, mv, w_co,
      g_ffn, w_gate, w_up, w_down, g_final)


def kernel(x_prompt, x_sample, mem_prompt, cache_sb_k, cache_sb_v, cache_mem_k, cache_mem_v,
           state_conv, page_table, g_mix, w_in, sb_bias, w_conv, g_attn_out, g_conv_out, w_out,
           g_mem, g_ca, w_cq, w_ck, w_cv, w_co, g_ffn, w_gate, w_up, w_down, g_final):
    depth = w_in.shape[0]
    assert depth == 1, "single-layer stack"
    batch, seq, _ = x_prompt.shape
    dec_batch, dec_seq, _ = x_sample.shape
    n_s = dec_batch * dec_seq
    assert seq % ROW_TILE == 0 and ROW_TILE % SB_BLOCK == 0 and n_s % SB_BLOCK == 0
    assert dec_seq == SUBLANES
    l = 0
    bf = lambda w: w[l].astype(BF16)
    vec = lambda g: g[l][None, :]
    w_out_b = bf(w_out)
    w_in_b = bf(w_in)
    w_rows = jnp.concatenate([w_in_b[:, :SB_WIDTH], w_in_b[:, 3 * SB_WIDTH:]], axis=1)
    w_kv_t = w_in_b[:, SB_WIDTH:3 * SB_WIDTH].T
    w_cq_b, w_ck_b, w_cv_b, w_co_b = bf(w_cq), bf(w_ck), bf(w_cv), bf(w_co)
    w_gate_b, w_up_b, w_down_b = bf(w_gate), bf(w_up), bf(w_down)
    g_fin = g_final[None, :]
    heads = lambda t: t.reshape(1, t.shape[0], SB_HEADS, SB_HEAD_DIM, t.shape[2]).transpose(0, 1, 4, 2, 3)

    def tail(x2, attn, bg, u, init, mk, mv, seq_len, rows_per_batch):
        h = _mixer_out(x2, attn, bg, u, init, w_conv[l], vec(g_attn_out), vec(g_conv_out),
                       w_out_b, seq_len)
        h = _cross(h, vec(g_ca), w_cq_b, mk, mv, w_co_b, rows_per_batch)
        return _ffn(h, vec(g_ffn), w_gate_b, w_up_b, w_down_b, g_fin)

    xp = x_prompt.reshape(batch * seq, D_MODEL)
    q, kt, vt, ktb, vtb, bg, u = _in_proj(xp, vec(g_mix), w_rows, w_kv_t, batch, seq)
    attn = _sb_prompt(q, ktb, vtb, sb_bias[l], batch, seq)
    mk, mv = _memory_kv(mem_prompt.reshape(batch * N_MEM, D_MODEL), vec(g_mem), w_ck_b, w_cv_b)
    init_p = jnp.zeros((batch, SUBLANES, CONV_CH), F32)
    y_prompt = _tail(xp, attn, bg, u, init_p, w_conv[l], vec(g_attn_out), vec(g_conv_out), w_out_b,
                     vec(g_ca), w_cq_b, mk, mv, w_co_b,
                     vec(g_ffn), w_gate_b, w_up_b, w_down_b, g_fin, seq).reshape(batch, seq, D_MODEL)
    conv_p = u.reshape(batch, seq, CONV_CH)[:, seq - (CONV_K - 1):][None]
    mem_shape = (1, batch, N_MEM, CA_HEADS, CA_HEAD_DIM)

    xs = x_sample.reshape(n_s, D_MODEL)
    qs, kts, vts, _, _, bgs, us = _in_proj(xs, vec(g_mix), w_rows, w_kv_t, 1, n_s)
    ks, vs = kts[0].T, vts[0].T
    attn_s = _sb_sample(qs.astype(F32), ks, vs, cache_sb_k[l], cache_sb_v[l],
                        page_table, sb_bias[l], dec_batch, dec_seq)
    init_s = jnp.pad(state_conv[l], ((0, 0), (SUBLANES - (CONV_K - 1), 0), (0, 0)))
    y_sample = tail(xs, attn_s, bgs, us, init_s,
                    cache_mem_k[l].reshape(dec_batch * N_MEM, CA_WIDTH),
                    cache_mem_v[l].reshape(dec_batch * N_MEM, CA_WIDTH),
                    dec_seq, dec_seq).reshape(dec_batch, dec_seq, D_MODEL)
    conv_s = us.reshape(dec_batch, dec_seq, CONV_CH)[:, dec_seq - (CONV_K - 1):][None]

    new_kv = lambda t: t.reshape(1, dec_batch, dec_seq, SB_HEADS, SB_HEAD_DIM)
    return (y_prompt, y_sample, heads(kt), heads(vt), conv_p,
            mk.reshape(mem_shape), mv.reshape(mem_shape), new_kv(ks), new_kv(vs), conv_s)
```

```python
import functools

import jax
import jax.numpy as jnp
from jax import lax
from jax.experimental import pallas as pl
from jax.experimental.pallas import tpu as pltpu

F32 = jnp.float32
BF16 = jnp.bfloat16

D_MODEL = 1024
SB_HEAD_DIM = 64
SB_WIDTH = 512
SB_HEADS = SB_WIDTH // SB_HEAD_DIM
SB_SCALE = SB_HEAD_DIM ** -0.5
CONV_CH = 512
CONV_K = 3
N_MEM = 256
CA_HEADS = 4
CA_HEAD_DIM = 128
CA_WIDTH = CA_HEADS * CA_HEAD_DIM
CA_SCALE = CA_HEAD_DIM ** -0.5
PAGE_SIZE = 128
EPS = 1e-6
LOG2E = 1.4426950408889634

LANES = 128
SUBLANES = 8
MXU_DIM = 256
VMEM_LIMIT_BYTES = 56 * 1024 * 1024

ROW_TILE = 512
SB_BLOCK = 256
FFN_CHUNK = 256
CROSS_BATCHES_PER_TILE = 8

SB_STAGE_ORDER = (
    ("logits", (0, 1)), ("log_terms", (0, 1)), ("logits", (2, 3)), ("suffixes", (0, 1)),
    ("log_terms", (2, 3)), ("weights", (0, 1)), ("suffixes", (2, 3)),
    ("values", (0, 1)), ("weights", (2, 3)), ("values", (2, 3)),
)


def _cparams(*sem):
    return pltpu.CompilerParams(dimension_semantics=sem, vmem_limit_bytes=VMEM_LIMIT_BYTES)


def _resident(shape):
    return pl.BlockSpec(shape, lambda *_: (0,) * len(shape), pipeline_mode=pl.Buffered(1))


def _rms(x, g):
    ms = jnp.mean(x * x, axis=-1, keepdims=True)
    return x * lax.rsqrt(ms + EPS) * g


def _dot(a, b):
    return jnp.dot(a, b, preferred_element_type=F32)


def _dot_nt(a, b):
    return lax.dot_general(a, b, (((1,), (1,)), ((), ())), preferred_element_type=F32)


def _log_sigmoids(z):
    soft = jnp.log(1.0 + jnp.exp2(jnp.abs(z) * -LOG2E))
    log_beta = jnp.minimum(z, 0.0) - soft
    return log_beta, log_beta - z


def _suffix_sums(log_om, tmat):
    return _dot(log_om.astype(BF16), tmat)


def _strict_lower_ones(n):
    row = lax.broadcasted_iota(jnp.int32, (n, n), 0)
    col = lax.broadcasted_iota(jnp.int32, (n, n), 1)
    return (row > col).astype(BF16)


def _in_proj_body(x_ref, g_ref, w_ref, wkv_ref, q_ref, kt_ref, vt_ref, ktb_ref, vtb_ref, bg_ref, u_ref):
    hn = _rms(x_ref[...], g_ref[...]).astype(BF16)

    def proj(c):
        return _dot(hn, w_ref[:, c * SB_WIDTH:(c + 1) * SB_WIDTH])

    q_ref[...] = (proj(0) * SB_SCALE).astype(BF16)
    bg_ref[...] = proj(1)
    u_ref[...] = proj(2) * proj(3)
    kv_t = _dot_nt(wkv_ref[...], hn)
    kt, vt = kv_t[:SB_WIDTH], kv_t[SB_WIDTH:]
    kt_ref[...] = kt
    vt_ref[...] = vt
    for c in range(ktb_ref.shape[0]):
        cols = slice(c * SB_BLOCK, (c + 1) * SB_BLOCK)
        ktb_ref[c] = kt[:, cols].astype(BF16)
        vtb_ref[c] = vt[:, cols].astype(BF16)


def _in_proj(x, g, w_rows, w_kv_t, batch, seq):
    n = batch * seq
    tm = min(ROW_TILE, seq)
    tiles = seq // tm
    blocks = tm // SB_BLOCK
    row = lambda width: pl.BlockSpec((tm, width), lambda b, i: (b * tiles + i, 0))
    t_spec = pl.BlockSpec((None, SB_WIDTH, tm), lambda b, i: (b, 0, i))
    tb_spec = pl.BlockSpec((None, blocks, SB_WIDTH, SB_BLOCK), lambda b, i: (b, i, 0, 0))
    rows = lambda dt: jax.ShapeDtypeStruct((n, SB_WIDTH), dt)
    t_shape = jax.ShapeDtypeStruct((batch, SB_WIDTH, seq), F32)
    tb_shape = jax.ShapeDtypeStruct((batch, seq // SB_BLOCK, SB_WIDTH, SB_BLOCK), BF16)
    return pl.pallas_call(
        _in_proj_body,
        grid=(batch, tiles),
        in_specs=[row(D_MODEL), _resident((1, D_MODEL)), _resident(w_rows.shape),
                  _resident(w_kv_t.shape)],
        out_specs=[row(SB_WIDTH), t_spec, t_spec, tb_spec, tb_spec, row(SB_WIDTH), row(SB_WIDTH)],
        out_shape=[rows(BF16), t_shape, t_shape, tb_shape, tb_shape, rows(F32), rows(F32)],
        compiler_params=_cparams("parallel", "parallel"),
        name="in_proj",
    )(x, g, w_rows, w_kv_t)


def _prompt_step(i, bias_ref, q_ref, k_ref, v_ref, o_ref, qm_ref, acc_ref, carry_ref):
    blk = SB_BLOCK
    n_pairs = SB_HEADS // 2
    row = lax.broadcasted_iota(jnp.int32, (2 * blk, blk), 0)
    col = lax.broadcasted_iota(jnp.int32, (2 * blk, blk), 1)
    readable = col < (row & (blk - 1))
    top = lax.broadcasted_iota(jnp.int32, (2 * blk, 1), 0) < blk
    tmat = _strict_lower_ones(blk)
    left = lax.broadcasted_iota(jnp.int32, (blk, LANES), 1) < SB_HEAD_DIM

    for pair in range(n_pairs):
        q2 = q_ref[:, pl.ds(pair * LANES, LANES)]
        zero = jnp.zeros_like(q2)
        qm_ref[pair] = jnp.concatenate(
            [jnp.where(left, q2, zero), jnp.where(left, zero, q2)], axis=0)

    def block(j, diagonal):
        kt_blk, vt_blk = k_ref[j], v_ref[j]
        carries = None if diagonal else [carry_ref[pair] for pair in range(n_pairs)]
        pairs = range(n_pairs)
        group = [slice(pair * LANES, (pair + 1) * LANES) for pair in pairs]
        qk, log_beta, log_om, suffix, a, outs, totals = ({} for _ in range(7))

        def logits(pair):
            qk[pair] = _dot(qm_ref[pair], kt_blk[group[pair]])

        def log_terms(pair):
            bias = jnp.where(top, bias_ref[2 * pair], bias_ref[2 * pair + 1])
            log_beta[pair], lo = _log_sigmoids(qk[pair] + bias)
            log_om[pair] = jnp.where(readable, lo, 0.0) if diagonal else lo
            totals[pair] = jnp.sum(log_om[pair], axis=1, keepdims=True)

        def suffixes(pair):
            suffix[pair] = _suffix_sums(log_om[pair], tmat)

        def weights(pair):
            if diagonal:
                a[pair] = jnp.where(readable, jnp.exp(log_beta[pair] + suffix[pair]), 0.0)
            else:
                a[pair] = jnp.exp(log_beta[pair] + suffix[pair] + carries[pair])

        def values(pair):
            outs[pair] = _dot_nt(a[pair].astype(BF16), vt_blk[group[pair]])

        for stage, members in SB_STAGE_ORDER:
            for pair in members:
                {"logits": logits, "log_terms": log_terms, "suffixes": suffixes,
                 "weights": weights, "values": values}[stage](pair)
        for pair in pairs:
            if diagonal:
                acc_ref[pair] = outs[pair]
                carry_ref[pair] = totals[pair]
            else:
                acc_ref[pair] += outs[pair]
                carry_ref[pair] = carries[pair] + totals[pair]

    block(i, True)

    @pl.loop(0, i)
    def _(jj):
        block(i - 1 - jj, False)

    for pair in range(n_pairs):
        o = acc_ref[pair]
        o_ref[:, pl.ds(pair * LANES, LANES)] = jnp.where(left, o[:blk], o[blk:])


def _sample_pages(g, n_groups, q_ref, kn_ref, vn_ref, brow_ref, k_pages, v_pages, o_ref,
                  qbd_ref, carry_ref, acc_ref):
    t_new = q_ref.shape[0]
    rows = SB_HEADS * t_new
    tmat = _strict_lower_ones(PAGE_SIZE)
    row_head = lax.broadcasted_iota(jnp.int32, (rows, SB_WIDTH), 0) // t_new
    col_head = lax.broadcasted_iota(jnp.int32, (rows, SB_WIDTH), 1) // SB_HEAD_DIM

    @pl.when(g == 0)
    def _():
        q_rows = jnp.concatenate([q_ref[...]] * SB_HEADS, axis=0)
        qbd = jnp.where(row_head == col_head, q_rows, 0.0).astype(BF16)
        qbd_ref[...] = qbd
        pad = jnp.zeros((PAGE_SIZE - t_new, SB_WIDTH), F32)
        k_new = jnp.concatenate([kn_ref[...], pad], axis=0).astype(BF16)
        v_new = jnp.concatenate([vn_ref[...], pad], axis=0).astype(BF16)
        key = lax.broadcasted_iota(jnp.int32, (rows, PAGE_SIZE), 1)
        readable = key < lax.broadcasted_iota(jnp.int32, (rows, PAGE_SIZE), 0) % t_new
        log_beta, log_om = _log_sigmoids(_dot_nt(qbd, k_new) + brow_ref[:, :PAGE_SIZE])
        log_om = jnp.where(readable, log_om, 0.0)
        a = jnp.where(readable, jnp.exp(log_beta + _suffix_sums(log_om, tmat)), 0.0)
        acc_ref[...] = _dot(a.astype(BF16), v_new)
        carry_ref[...] = jnp.broadcast_to(jnp.sum(log_om, axis=1, keepdims=True), carry_ref.shape)

    kt = jnp.concatenate([k[...] for k in k_pages], axis=1).astype(BF16)
    vt = jnp.concatenate([v[...] for v in v_pages], axis=1).astype(BF16)
    log_beta, log_om = _log_sigmoids(_dot(qbd_ref[...], kt) + brow_ref[...])
    carry = carry_ref[...]
    a = []
    for s in range(len(k_pages)):
        cols = slice(s * PAGE_SIZE, (s + 1) * PAGE_SIZE)
        a.append(jnp.exp(log_beta[:, cols] + _suffix_sums(log_om[:, cols], tmat) + carry))
        carry = carry + jnp.sum(log_om[:, cols], axis=1, keepdims=True)
    carry_ref[...] = carry
    acc = acc_ref[...] + _dot_nt(jnp.concatenate(a, axis=1).astype(BF16), vt)
    acc_ref[...] = acc

    @pl.when(g == n_groups - 1)
    def _():
        o = jnp.zeros((t_new, SB_WIDTH), F32)
        for head in range(SB_HEADS):
            band = acc[head * t_new:(head + 1) * t_new]
            o = o + jnp.where(col_head[:t_new] == head, band, 0.0)
        o_ref[...] = o


def _sb_body(pt_ref, bias_ref, q_ref, k_ref, v_ref, qs_ref, kn_ref, vn_ref, brow_ref, *refs,
             pps, n_groups):
    del pt_ref
    k_pages, v_pages = refs[:pps], refs[pps:2 * pps]
    o_ref, os_ref, qm_ref, acc_ref, carry_ref, qbd_ref, scarry_ref, sacc_ref = refs[2 * pps:]
    step = pl.program_id(0) * pl.num_programs(1) + pl.program_id(1)
    _sample_pages(lax.rem(step, n_groups), n_groups, qs_ref, kn_ref, vn_ref, brow_ref,
                  k_pages, v_pages, os_ref, qbd_ref, scarry_ref, sacc_ref)
    _prompt_step(pl.program_id(1), bias_ref, q_ref, k_ref, v_ref, o_ref, qm_ref, acc_ref, carry_ref)


def _sb_attention(q, ktb, vtb, bias, batch, seq, qs, ks_new, vs_new, cache_k, cache_v, page_table):
    nq = seq // SB_BLOCK
    n_pairs = SB_HEADS // 2
    dec_batch, n_pages = page_table.shape
    t_new = qs.shape[0] // dec_batch
    steps = batch * nq
    assert steps % dec_batch == 0 and (dec_batch * n_pages) % steps == 0
    n_groups = steps // dec_batch
    pps = n_pages // n_groups
    rows = SB_HEADS * t_new
    brow = jnp.broadcast_to(jnp.repeat(bias, t_new)[:, None], (rows, pps * PAGE_SIZE))
    keys_minor = lambda c: c.transpose(0, 2, 3, 1).reshape(c.shape[0], SB_WIDTH, PAGE_SIZE)
    seq_of = lambda b, i: lax.div(b * nq + i, n_groups)

    def page_spec(slot):
        def index(b, i, pt):
            step = b * nq + i
            logical = n_pages - 1 - slot - lax.rem(step, n_groups) * pps
            return (pt[lax.div(step, n_groups) * n_pages + logical], 0, 0)
        return pl.BlockSpec((None, SB_WIDTH, PAGE_SIZE), index)

    q_spec = pl.BlockSpec((SB_BLOCK, SB_WIDTH), lambda b, i, pt: (b * nq + i, 0))
    kv_spec = pl.BlockSpec((None, nq, SB_WIDTH, SB_BLOCK), lambda b, i, pt: (b, 0, 0, 0))
    tok_spec = pl.BlockSpec((t_new, SB_WIDTH), lambda b, i, pt: (seq_of(b, i), 0))
    grid_spec = pltpu.PrefetchScalarGridSpec(
        num_scalar_prefetch=1,
        grid=(batch, nq),
        in_specs=[pl.BlockSpec(memory_space=pltpu.SMEM), q_spec, kv_spec, kv_spec,
                  tok_spec, tok_spec, tok_spec, pl.BlockSpec(brow.shape, lambda b, i, pt: (0, 0))]
                 + [page_spec(slot) for slot in range(pps)] * 2,
        out_specs=[q_spec, tok_spec],
        scratch_shapes=[pltpu.VMEM((n_pairs, 2 * SB_BLOCK, LANES), BF16),
                        pltpu.VMEM((n_pairs, 2 * SB_BLOCK, LANES), F32),
                        pltpu.VMEM((n_pairs, 2 * SB_BLOCK, 1), F32),
                        pltpu.VMEM((rows, SB_WIDTH), BF16),
                        pltpu.VMEM((rows, PAGE_SIZE), F32),
                        pltpu.VMEM((rows, SB_WIDTH), F32)],
    )
    return pl.pallas_call(
        functools.partial(_sb_body, pps=pps, n_groups=n_groups),
        grid_spec=grid_spec,
        out_shape=[jax.ShapeDtypeStruct((batch * seq, SB_WIDTH), F32),
                   jax.ShapeDtypeStruct(qs.shape, F32)],
        compiler_params=_cparams("arbitrary", "arbitrary"),
        name="sb_attention",
    )(page_table.reshape(-1), bias, q, ktb, vtb, qs, ks_new, vs_new, brow,
      *([keys_minor(cache_k)] * pps), *([keys_minor(cache_v)] * pps))


def _mixer_math(x_ref, a_ref, bg_ref, u_ref, uprev_ref, init_ref, wc_ref, ga_ref, gc_ref,
                wo_ref, ext_ref, tiles_per_seq):
    tm = u_ref.shape[0]
    i = pl.program_id(0)
    if tiles_per_seq == 1:
        halo = init_ref[0]
    else:
        halo = jnp.where(lax.rem(i, tiles_per_seq) == 0, init_ref[0], uprev_ref[...])
    u = u_ref[...]
    ext_ref[0:SUBLANES, :] = halo
    ext_ref[SUBLANES:SUBLANES + tm, :] = u
    u1 = ext_ref[SUBLANES - 1:SUBLANES - 1 + tm, :]
    u2 = ext_ref[SUBLANES - 2:SUBLANES - 2 + tm, :]
    wc = wc_ref[...]
    conv = wc[0:1] * u2 + wc[1:2] * u1 + wc[2:3] * u
    c = _rms(bg_ref[...] * conv, gc_ref[...]).astype(BF16)
    a = _rms(a_ref[...], ga_ref[...]).astype(BF16)
    return (x_ref[...] + _dot(a, wo_ref[0:SB_WIDTH, :])
            + _dot(c, wo_ref[SB_WIDTH:SB_WIDTH + CONV_CH, :]))


def _mixer_out_body(*refs, tiles_per_seq):
    *ins, h_ref, ext_ref = refs
    h_ref[...] = _mixer_math(*ins, ext_ref, tiles_per_seq)


def _mixer_specs(tm, tiles_per_seq, w_out_shape):
    per8 = tm // SUBLANES
    row = lambda width: pl.BlockSpec((tm, width), lambda i: (i, 0))
    return [
        row(D_MODEL), row(SB_WIDTH), row(CONV_CH), row(CONV_CH),
        pl.BlockSpec((SUBLANES, CONV_CH), lambda i: (jnp.maximum(i * per8 - 1, 0), 0)),
        pl.BlockSpec((1, SUBLANES, CONV_CH), lambda i: (i // tiles_per_seq, 0, 0)),
        _resident((CONV_K, CONV_CH)), _resident((1, SB_WIDTH)), _resident((1, CONV_CH)),
        _resident(w_out_shape),
    ]


def _mixer_out(x, attn, bg, u, init, w_conv, g_attn, g_conv, w_out, seq):
    n = x.shape[0]
    tm = min(ROW_TILE, seq)
    tiles_per_seq = seq // tm
    return pl.pallas_call(
        functools.partial(_mixer_out_body, tiles_per_seq=tiles_per_seq),
        grid=(n // tm,),
        in_specs=_mixer_specs(tm, tiles_per_seq, w_out.shape),
        out_specs=pl.BlockSpec((tm, D_MODEL), lambda i: (i, 0)),
        out_shape=jax.ShapeDtypeStruct((n, D_MODEL), F32),
        scratch_shapes=[pltpu.VMEM((tm + SUBLANES, CONV_CH), F32)],
        compiler_params=_cparams("parallel"),
        name="mixer_out",
    )(x, attn, bg, u, u, init, w_conv, g_attn, g_conv, w_out)


def _memory_kv_body(m_ref, g_ref, wk_ref, wv_ref, k_ref, v_ref):
    mn = _rms(m_ref[...], g_ref[...]).astype(BF16)
    k_ref[...] = _dot(mn, wk_ref[...])
    v_ref[...] = _dot(mn, wv_ref[...])


def _memory_kv(mem, g, w_ck, w_cv):
    n = mem.shape[0]
    tm = min(ROW_TILE, n)
    row = lambda width: pl.BlockSpec((tm, width), lambda i: (i, 0))
    return pl.pallas_call(
        _memory_kv_body,
        grid=(n // tm,),
        in_specs=[row(D_MODEL), _resident((1, D_MODEL)), _resident(w_ck.shape), _resident(w_cv.shape)],
        out_specs=[row(CA_WIDTH)] * 2,
        out_shape=[jax.ShapeDtypeStruct((n, CA_WIDTH), F32)] * 2,
        compiler_params=_cparams("parallel"),
        name="memory_kv",
    )(mem, g, w_ck, w_cv)


def _cross_math(h, g_ref, wq_ref, mk_ref, mv_ref, wo_ref):
    n_b = mk_ref.shape[0] // N_MEM
    r = h.shape[0] // n_b
    q = _dot(_rms(h, g_ref[...]).astype(BF16), wq_ref[...])
    per_batch = []
    for b in range(n_b):
        mem_rows = pl.ds(b * N_MEM, N_MEM)
        outs = []
        for head in range(CA_HEADS):
            lanes = pl.ds(head * CA_HEAD_DIM, CA_HEAD_DIM)
            qh = q[b * r:(b + 1) * r, head * CA_HEAD_DIM:(head + 1) * CA_HEAD_DIM].astype(BF16)
            sc = _dot_nt(qh, mk_ref[mem_rows, lanes].astype(BF16)) * CA_SCALE
            p = jnp.exp(sc - jnp.max(sc, axis=-1, keepdims=True))
            denom = jnp.sum(p, axis=-1, keepdims=True)
            outs.append(_dot(p.astype(BF16), mv_ref[mem_rows, lanes].astype(BF16)) / denom)
        per_batch.append(jnp.concatenate(outs, axis=-1))
    o = jnp.concatenate(per_batch, axis=0).astype(BF16)
    return h + _dot(o, wo_ref[...])


def _cross_body(h_ref, *refs):
    *ins, o_ref = refs
    o_ref[...] = _cross_math(h_ref[...], *ins)


def _cross(h, g, w_cq, mk, mv, w_co, rows_per_batch):
    n = h.shape[0]
    assert rows_per_batch < ROW_TILE
    n_b = min(CROSS_BATCHES_PER_TILE, n // rows_per_batch)
    tm = n_b * rows_per_batch
    row = pl.BlockSpec((tm, D_MODEL), lambda i: (i, 0))
    mem = pl.BlockSpec((n_b * N_MEM, CA_WIDTH), lambda i: (i, 0))
    return pl.pallas_call(
        _cross_body,
        grid=(n // tm,),
        in_specs=[row, _resident((1, D_MODEL)), _resident(w_cq.shape), mem, mem,
                  _resident(w_co.shape)],
        out_specs=row,
        out_shape=jax.ShapeDtypeStruct((n, D_MODEL), F32),
        compiler_params=_cparams("parallel"),
        name="cross_attn",
    )(h, g, w_cq, mk, mv, w_co)


def _ffn_math(h, g_ref, wg_ref, wu_ref, wd_ref, gf_ref):
    hn = _rms(h, g_ref[...]).astype(BF16)
    acc = h
    for c in range(0, wg_ref.shape[1], FFN_CHUNK):
        gate = _dot(hn, wg_ref[:, c:c + FFN_CHUNK])
        up = _dot(hn, wu_ref[:, c:c + FFN_CHUNK])
        act = (gate / (1.0 + jnp.exp(-gate)) * up).astype(BF16)
        acc = acc + _dot(act, wd_ref[c:c + FFN_CHUNK, :])
    return _rms(acc, gf_ref[...])


def _ffn_body(h_ref, *refs):
    *ins, y_ref = refs
    y_ref[...] = _ffn_math(h_ref[...], *ins)


def _ffn(h, g, w_gate, w_up, w_down, g_final):
    n = h.shape[0]
    tm = min(ROW_TILE, n)
    row = pl.BlockSpec((tm, D_MODEL), lambda i: (i, 0))
    return pl.pallas_call(
        _ffn_body,
        grid=(n // tm,),
        in_specs=[row, _resident((1, D_MODEL)), _resident(w_gate.shape), _resident(w_up.shape),
                  _resident(w_down.shape), _resident((1, D_MODEL))],
        out_specs=row,
        out_shape=jax.ShapeDtypeStruct((n, D_MODEL), F32),
        compiler_params=_cparams("parallel"),
        name="ffn",
    )(h, g, w_gate, w_up, w_down, g_final)


N_MIXER_INS, N_CROSS_INS = 10, 5


def _tail_body(*refs, tiles_per_seq):
    *ins, y_ref, ext_ref = refs
    mixer_ins, ins = ins[:N_MIXER_INS], ins[N_MIXER_INS:]
    cross_ins, ffn_ins = ins[:N_CROSS_INS], ins[N_CROSS_INS:]
    h = _mixer_math(*mixer_ins, ext_ref, tiles_per_seq)
    h = _cross_math(h, *cross_ins)
    y_ref[...] = _ffn_math(h, *ffn_ins)


def _tail(x, attn, bg, u, init, w_conv, g_attn, g_conv, w_out, g_ca, w_cq, mk, mv, w_co,
          g_ffn, w_gate, w_up, w_down, g_final, seq):
    n = x.shape[0]
    tm = min(ROW_TILE, seq)
    tiles_per_seq = seq // tm
    vec_spec = _resident((1, D_MODEL))
    mem = pl.BlockSpec((N_MEM, CA_WIDTH), lambda i: (i // tiles_per_seq, 0))
    return pl.pallas_call(
        functools.partial(_tail_body, tiles_per_seq=tiles_per_seq),
        grid=(n // tm,),
        in_specs=_mixer_specs(tm, tiles_per_seq, w_out.shape)
                 + [vec_spec, _resident(w_cq.shape), mem, mem, _resident(w_co.shape)]
                 + [vec_spec, _resident(w_gate.shape), _resident(w_up.shape),
                    _resident(w_down.shape), vec_spec],
        out_specs=pl.BlockSpec((tm, D_MODEL), lambda i: (i, 0)),
        out_shape=jax.ShapeDtypeStruct((n, D_MODEL), F32),
        scratch_shapes=[pltpu.VMEM((tm + SUBLANES, CONV_CH), F32)],
        compiler_params=_cparams("parallel"),
        name="tail",
    )(x, attn, bg, u, u, init, w_conv, g_attn, g_conv, w_out, g_ca, w_cq, mk, mv, w_co,
      g_ffn, w_gate, w_up, w_down, g_final)


def kernel(x_prompt, x_sample, mem_prompt, cache_sb_k, cache_sb_v, cache_mem_k, cache_mem_v,
           state_conv, page_table, g_mix, w_in, sb_bias, w_conv, g_attn_out, g_conv_out, w_out,
           g_mem, g_ca, w_cq, w_ck, w_cv, w_co, g_ffn, w_gate, w_up, w_down, g_final):
    depth = w_in.shape[0]
    assert depth == 1, "single-layer stack"
    batch, seq, _ = x_prompt.shape
    dec_batch, dec_seq, _ = x_sample.shape
    n_s = dec_batch * dec_seq
    assert seq % ROW_TILE == 0 and ROW_TILE % SB_BLOCK == 0 and n_s % SB_BLOCK == 0
    assert dec_seq == SUBLANES
    l = 0
    bf = lambda w: w[l].astype(BF16)
    vec = lambda g: g[l][None, :]
    w_out_b = bf(w_out)
    w_in_b = bf(w_in)
    w_rows = jnp.concatenate([w_in_b[:, :SB_WIDTH], w_in_b[:, 3 * SB_WIDTH:]], axis=1)
    w_kv_t = w_in_b[:, SB_WIDTH:3 * SB_WIDTH].T
    w_cq_b, w_ck_b, w_cv_b, w_co_b = bf(w_cq), bf(w_ck), bf(w_cv), bf(w_co)
    w_gate_b, w_up_b, w_down_b = bf(w_gate), bf(w_up), bf(w_down)
    g_fin = g_final[None, :]
    heads = lambda t: t.reshape(1, t.shape[0], SB_HEADS, SB_HEAD_DIM, t.shape[2]).transpose(0, 1, 4, 2, 3)

    def tail(x2, attn, bg, u, init, mk, mv, seq_len, rows_per_batch):
        h = _mixer_out(x2, attn, bg, u, init, w_conv[l], vec(g_attn_out), vec(g_conv_out),
                       w_out_b, seq_len)
        h = _cross(h, vec(g_ca), w_cq_b, mk, mv, w_co_b, rows_per_batch)
        return _ffn(h, vec(g_ffn), w_gate_b, w_up_b, w_down_b, g_fin)

    xp = x_prompt.reshape(batch * seq, D_MODEL)
    xs = x_sample.reshape(n_s, D_MODEL)
    q, kt, vt, ktb, vtb, bg, u = _in_proj(xp, vec(g_mix), w_rows, w_kv_t, batch, seq)
    qs, kts, vts, _, _, bgs, us = _in_proj(xs, vec(g_mix), w_rows, w_kv_t, 1, n_s)
    ks, vs = kts[0].T, vts[0].T
    attn, attn_s = _sb_attention(q, ktb, vtb, sb_bias[l], batch, seq, qs.astype(F32), ks, vs,
                                 cache_sb_k[l], cache_sb_v[l], page_table)

    mk, mv = _memory_kv(mem_prompt.reshape(batch * N_MEM, D_MODEL), vec(g_mem), w_ck_b, w_cv_b)
    init_p = jnp.zeros((batch, SUBLANES, CONV_CH), F32)
    y_prompt = _tail(xp, attn, bg, u, init_p, w_conv[l], vec(g_attn_out), vec(g_conv_out), w_out_b,
                     vec(g_ca), w_cq_b, mk, mv, w_co_b,
                     vec(g_ffn), w_gate_b, w_up_b, w_down_b, g_fin, seq).reshape(batch, seq, D_MODEL)
    conv_p = u.reshape(batch, seq, CONV_CH)[:, seq - (CONV_K - 1):][None]
    mem_shape = (1, batch, N_MEM, CA_HEADS, CA_HEAD_DIM)

    init_s = jnp.pad(state_conv[l], ((0, 0), (SUBLANES - (CONV_K - 1), 0), (0, 0)))
    y_sample = tail(xs, attn_s, bgs, us, init_s,
                    cache_mem_k[l].reshape(dec_batch * N_MEM, CA_WIDTH),
                    cache_mem_v[l].reshape(dec_batch * N_MEM, CA_WIDTH),
                    dec_seq, dec_seq).reshape(dec_batch, dec_seq, D_MODEL)
    conv_s = us.reshape(dec_batch, dec_seq, CONV_CH)[:, dec_seq - (CONV_K - 1):][None]

    new_kv = lambda t: t.reshape(1, dec_batch, dec_seq, SB_HEADS, SB_HEAD_DIM)
    return (y_prompt, y_sample, heads(kt), heads(vt), conv_p,
            mk.reshape(mem_shape), mv.reshape(mem_shape), new_kv(ks), new_kv(vs), conv_s)
```

```python
import functools

import jax
import jax.numpy as jnp
from jax import lax
from jax.experimental import pallas as pl
from jax.experimental.pallas import tpu as pltpu

F32 = jnp.float32
BF16 = jnp.bfloat16

D_MODEL = 1024
SB_HEAD_DIM = 64
SB_WIDTH = 512
SB_HEADS = SB_WIDTH // SB_HEAD_DIM
SB_SCALE = SB_HEAD_DIM ** -0.5
CONV_CH = 512
CONV_K = 3
N_MEM = 256
CA_HEADS = 4
CA_HEAD_DIM = 128
CA_WIDTH = CA_HEADS * CA_HEAD_DIM
CA_SCALE = CA_HEAD_DIM ** -0.5
PAGE_SIZE = 128
EPS = 1e-6
LOG2E = 1.4426950408889634

LANES = 128
SUBLANES = 8
MXU_DIM = 256
VMEM_LIMIT_BYTES = 56 * 1024 * 1024

ROW_TILE = 512
IN_PROJ_TILE = 1024
SB_BLOCK = 256
FFN_CHUNK = 256
CROSS_BATCHES_PER_TILE = 8

SB_STAGE_ORDER = (
    ("logits", (0, 1)), ("log_terms", (0, 1)), ("logits", (2, 3)), ("suffixes", (0, 1)),
    ("log_terms", (2, 3)), ("weights", (0, 1)), ("suffixes", (2, 3)),
    ("values", (0, 1)), ("weights", (2, 3)), ("values", (2, 3)),
)


def _cparams(*sem):
    return pltpu.CompilerParams(dimension_semantics=sem, vmem_limit_bytes=VMEM_LIMIT_BYTES)


def _resident(shape):
    return pl.BlockSpec(shape, lambda *_: (0,) * len(shape), pipeline_mode=pl.Buffered(1))


def _rms(x, g):
    ms = jnp.mean(x * x, axis=-1, keepdims=True)
    return x * lax.rsqrt(ms + EPS) * g


def _dot(a, b):
    return jnp.dot(a, b, preferred_element_type=F32)


def _dot_nt(a, b):
    return lax.dot_general(a, b, (((1,), (1,)), ((), ())), preferred_element_type=F32)


def _log_sigmoids(z):
    soft = jnp.log(1.0 + jnp.exp2(jnp.abs(z) * -LOG2E))
    log_beta = jnp.minimum(z, 0.0) - soft
    return log_beta, log_beta - z


def _suffix_sums(log_om, tmat):
    return _dot(log_om.astype(BF16), tmat)


def _strict_lower_ones(n):
    row = lax.broadcasted_iota(jnp.int32, (n, n), 0)
    col = lax.broadcasted_iota(jnp.int32, (n, n), 1)
    return (row > col).astype(BF16)


def _in_proj_body(x_ref, g_ref, w_ref, wkv_ref, q_ref, kt_ref, vt_ref, ktb_ref, vtb_ref, bg_ref, u_ref):
    hn = _rms(x_ref[...], g_ref[...]).astype(BF16)

    def proj(c):
        return _dot(hn, w_ref[:, c * SB_WIDTH:(c + 1) * SB_WIDTH])

    q_ref[...] = (proj(0) * SB_SCALE).astype(BF16)
    bg_ref[...] = proj(1)
    u_ref[...] = proj(2) * proj(3)
    kv_t = _dot_nt(wkv_ref[...], hn)
    kt, vt = kv_t[:SB_WIDTH], kv_t[SB_WIDTH:]
    kt_ref[...] = kt
    vt_ref[...] = vt
    for c in range(ktb_ref.shape[0]):
        cols = slice(c * SB_BLOCK, (c + 1) * SB_BLOCK)
        ktb_ref[c] = kt[:, cols].astype(BF16)
        vtb_ref[c] = vt[:, cols].astype(BF16)


def _in_proj(x, g, w_rows, w_kv_t, batch, seq):
    n = batch * seq
    tm = min(IN_PROJ_TILE, seq)
    tiles = seq // tm
    blocks = tm // SB_BLOCK
    row = lambda width: pl.BlockSpec((tm, width), lambda b, i: (b * tiles + i, 0))
    t_spec = pl.BlockSpec((None, SB_WIDTH, tm), lambda b, i: (b, 0, i))
    tb_spec = pl.BlockSpec((None, blocks, SB_WIDTH, SB_BLOCK), lambda b, i: (b, i, 0, 0))
    rows = lambda dt: jax.ShapeDtypeStruct((n, SB_WIDTH), dt)
    t_shape = jax.ShapeDtypeStruct((batch, SB_WIDTH, seq), F32)
    tb_shape = jax.ShapeDtypeStruct((batch, seq // SB_BLOCK, SB_WIDTH, SB_BLOCK), BF16)
    return pl.pallas_call(
        _in_proj_body,
        grid=(batch, tiles),
        in_specs=[row(D_MODEL), _resident((1, D_MODEL)), _resident(w_rows.shape),
                  _resident(w_kv_t.shape)],
        out_specs=[row(SB_WIDTH), t_spec, t_spec, tb_spec, tb_spec, row(SB_WIDTH), row(SB_WIDTH)],
        out_shape=[rows(BF16), t_shape, t_shape, tb_shape, tb_shape, rows(F32), rows(F32)],
        compiler_params=_cparams("parallel", "parallel"),
        name="in_proj",
    )(x, g, w_rows, w_kv_t)


def _prompt_step(i, bias_ref, q_ref, k_ref, v_ref, o_ref, qm_ref, acc_ref, carry_ref):
    blk = SB_BLOCK
    n_pairs = SB_HEADS // 2
    row = lax.broadcasted_iota(jnp.int32, (2 * blk, blk), 0)
    col = lax.broadcasted_iota(jnp.int32, (2 * blk, blk), 1)
    readable = col < (row & (blk - 1))
    top = lax.broadcasted_iota(jnp.int32, (2 * blk, 1), 0) < blk
    tmat = _strict_lower_ones(blk)
    left = lax.broadcasted_iota(jnp.int32, (blk, LANES), 1) < SB_HEAD_DIM

    for pair in range(n_pairs):
        q2 = q_ref[:, pl.ds(pair * LANES, LANES)]
        zero = jnp.zeros_like(q2)
        qm_ref[pair] = jnp.concatenate(
            [jnp.where(left, q2, zero), jnp.where(left, zero, q2)], axis=0)

    def block(j, diagonal):
        kt_blk, vt_blk = k_ref[j], v_ref[j]
        carries = None if diagonal else [carry_ref[pair] for pair in range(n_pairs)]
        pairs = range(n_pairs)
        group = [slice(pair * LANES, (pair + 1) * LANES) for pair in pairs]
        qk, log_beta, log_om, suffix, a, outs, totals = ({} for _ in range(7))

        def logits(pair):
            qk[pair] = _dot(qm_ref[pair], kt_blk[group[pair]])

        def log_terms(pair):
            bias = jnp.where(top, bias_ref[2 * pair], bias_ref[2 * pair + 1])
            log_beta[pair], lo = _log_sigmoids(qk[pair] + bias)
            log_om[pair] = jnp.where(readable, lo, 0.0) if diagonal else lo
            totals[pair] = jnp.sum(log_om[pair], axis=1, keepdims=True)

        def suffixes(pair):
            suffix[pair] = _suffix_sums(log_om[pair], tmat)

        def weights(pair):
            if diagonal:
                a[pair] = jnp.where(readable, jnp.exp(log_beta[pair] + suffix[pair]), 0.0)
            else:
                a[pair] = jnp.exp(log_beta[pair] + suffix[pair] + carries[pair])

        def values(pair):
            outs[pair] = _dot_nt(a[pair].astype(BF16), vt_blk[group[pair]])

        for stage, members in SB_STAGE_ORDER:
            for pair in members:
                {"logits": logits, "log_terms": log_terms, "suffixes": suffixes,
                 "weights": weights, "values": values}[stage](pair)
        for pair in pairs:
            if diagonal:
                acc_ref[pair] = outs[pair]
                carry_ref[pair] = totals[pair]
            else:
                acc_ref[pair] += outs[pair]
                carry_ref[pair] = carries[pair] + totals[pair]

    block(i, True)

    @pl.loop(0, i)
    def _(jj):
        block(i - 1 - jj, False)

    for pair in range(n_pairs):
        o = acc_ref[pair]
        o_ref[:, pl.ds(pair * LANES, LANES)] = jnp.where(left, o[:blk], o[blk:])


def _sample_pages(g, q_ref, kn_ref, vn_ref, brow_ref, k_pages, v_pages, qbd_ref, carry_ref, acc_ref):
    t_new = q_ref.shape[0]
    rows = SB_HEADS * t_new
    tmat = _strict_lower_ones(PAGE_SIZE)
    row_head = lax.broadcasted_iota(jnp.int32, (rows, SB_WIDTH), 0) // t_new
    col_head = lax.broadcasted_iota(jnp.int32, (rows, SB_WIDTH), 1) // SB_HEAD_DIM

    @pl.when(g == 0)
    def _():
        q_rows = jnp.concatenate([q_ref[...]] * SB_HEADS, axis=0)
        qbd = jnp.where(row_head == col_head, q_rows, 0.0).astype(BF16)
        qbd_ref[...] = qbd
        pad = jnp.zeros((PAGE_SIZE - t_new, SB_WIDTH), F32)
        k_new = jnp.concatenate([kn_ref[...], pad], axis=0).astype(BF16)
        v_new = jnp.concatenate([vn_ref[...], pad], axis=0).astype(BF16)
        key = lax.broadcasted_iota(jnp.int32, (rows, PAGE_SIZE), 1)
        readable = key < lax.broadcasted_iota(jnp.int32, (rows, PAGE_SIZE), 0) % t_new
        log_beta, log_om = _log_sigmoids(_dot_nt(qbd, k_new) + brow_ref[:, :PAGE_SIZE])
        log_om = jnp.where(readable, log_om, 0.0)
        a = jnp.where(readable, jnp.exp(log_beta + _suffix_sums(log_om, tmat)), 0.0)
        acc_ref[...] = _dot(a.astype(BF16), v_new)
        carry_ref[...] = jnp.broadcast_to(jnp.sum(log_om, axis=1, keepdims=True), carry_ref.shape)

    kt = jnp.concatenate([k[...] for k in k_pages], axis=1).astype(BF16)
    vt = jnp.concatenate([v[...] for v in v_pages], axis=1).astype(BF16)
    log_beta, log_om = _log_sigmoids(_dot(qbd_ref[...], kt) + brow_ref[...])
    carry = carry_ref[...]
    a = []
    for s in range(len(k_pages)):
        cols = slice(s * PAGE_SIZE, (s + 1) * PAGE_SIZE)
        a.append(jnp.exp(log_beta[:, cols] + _suffix_sums(log_om[:, cols], tmat) + carry))
        carry = carry + jnp.sum(log_om[:, cols], axis=1, keepdims=True)
    carry_ref[...] = carry
    acc_ref[...] += _dot_nt(jnp.concatenate(a, axis=1).astype(BF16), vt)


def _sample_output(acc_ref, o_ref):
    t_new = o_ref.shape[0]
    acc = acc_ref[...]
    col_head = lax.broadcasted_iota(jnp.int32, (t_new, SB_WIDTH), 1) // SB_HEAD_DIM
    o = jnp.zeros((t_new, SB_WIDTH), F32)
    for head in range(SB_HEADS):
        o = o + jnp.where(col_head == head, acc[head * t_new:(head + 1) * t_new], 0.0)
    o_ref[...] = o


def _sb_body(pt_ref, bias_ref, q_ref, k_ref, v_ref, qs_ref, kn_ref, vn_ref, brow_ref, *refs,
             pps, n_groups):
    del pt_ref
    k_pages, v_pages = refs[:pps], refs[pps:2 * pps]
    o_ref, os_ref, qm_ref, acc_ref, carry_ref, qbd_ref, scarry_ref, sacc_ref = refs[2 * pps:]
    step = pl.program_id(0) * pl.num_programs(1) + pl.program_id(1)
    g = lax.rem(step, n_groups)
    _sample_pages(g, qs_ref, kn_ref, vn_ref, brow_ref, k_pages, v_pages, qbd_ref, scarry_ref, sacc_ref)
    _prompt_step(pl.program_id(1), bias_ref, q_ref, k_ref, v_ref, o_ref, qm_ref, acc_ref, carry_ref)

    @pl.when(g == n_groups - 1)
    def _():
        _sample_output(sacc_ref, os_ref)


def _sb_attention(q, ktb, vtb, bias, batch, seq, qs, ks_new, vs_new, cache_k, cache_v, page_table):
    nq = seq // SB_BLOCK
    n_pairs = SB_HEADS // 2
    dec_batch, n_pages = page_table.shape
    t_new = qs.shape[0] // dec_batch
    steps = batch * nq
    assert steps % dec_batch == 0 and (dec_batch * n_pages) % steps == 0
    n_groups = steps // dec_batch
    pps = n_pages // n_groups
    rows = SB_HEADS * t_new
    brow = jnp.broadcast_to(jnp.repeat(bias, t_new)[:, None], (rows, pps * PAGE_SIZE))
    keys_minor = lambda c: c.transpose(0, 2, 3, 1).reshape(c.shape[0], SB_WIDTH, PAGE_SIZE)
    seq_of = lambda b, i: lax.div(b * nq + i, n_groups)

    def page_spec(slot):
        def index(b, i, pt):
            step = b * nq + i
            logical = n_pages - 1 - slot - lax.rem(step, n_groups) * pps
            return (pt[lax.div(step, n_groups) * n_pages + logical], 0, 0)
        return pl.BlockSpec((None, SB_WIDTH, PAGE_SIZE), index)

    q_spec = pl.BlockSpec((SB_BLOCK, SB_WIDTH), lambda b, i, pt: (b * nq + i, 0))
    kv_spec = pl.BlockSpec((None, nq, SB_WIDTH, SB_BLOCK), lambda b, i, pt: (b, 0, 0, 0))
    tok_spec = pl.BlockSpec((t_new, SB_WIDTH), lambda b, i, pt: (seq_of(b, i), 0))
    grid_spec = pltpu.PrefetchScalarGridSpec(
        num_scalar_prefetch=1,
        grid=(batch, nq),
        in_specs=[pl.BlockSpec(memory_space=pltpu.SMEM), q_spec, kv_spec, kv_spec,
                  tok_spec, tok_spec, tok_spec, pl.BlockSpec(brow.shape, lambda b, i, pt: (0, 0))]
                 + [page_spec(slot) for slot in range(pps)] * 2,
        out_specs=[q_spec, tok_spec],
        scratch_shapes=[pltpu.VMEM((n_pairs, 2 * SB_BLOCK, LANES), BF16),
                        pltpu.VMEM((n_pairs, 2 * SB_BLOCK, LANES), F32),
                        pltpu.VMEM((n_pairs, 2 * SB_BLOCK, 1), F32),
                        pltpu.VMEM((rows, SB_WIDTH), BF16),
                        pltpu.VMEM((rows, PAGE_SIZE), F32),
                        pltpu.VMEM((rows, SB_WIDTH), F32)],
    )
    return pl.pallas_call(
        functools.partial(_sb_body, pps=pps, n_groups=n_groups),
        grid_spec=grid_spec,
        out_shape=[jax.ShapeDtypeStruct((batch * seq, SB_WIDTH), F32),
                   jax.ShapeDtypeStruct(qs.shape, F32)],
        compiler_params=_cparams("arbitrary", "arbitrary"),
        name="sb_attention",
    )(page_table.reshape(-1), bias, q, ktb, vtb, qs, ks_new, vs_new, brow,
      *([keys_minor(cache_k)] * pps), *([keys_minor(cache_v)] * pps))


def _mixer_math(x_ref, a_ref, bg_ref, u_ref, uprev_ref, init_ref, wc_ref, ga_ref, gc_ref,
                wo_ref, ext_ref, tiles_per_seq):
    tm = u_ref.shape[0]
    i = pl.program_id(0)
    if tiles_per_seq == 1:
        halo = init_ref[0]
    else:
        halo = jnp.where(lax.rem(i, tiles_per_seq) == 0, init_ref[0], uprev_ref[...])
    u = u_ref[...]
    ext_ref[0:SUBLANES, :] = halo
    ext_ref[SUBLANES:SUBLANES + tm, :] = u
    u1 = ext_ref[SUBLANES - 1:SUBLANES - 1 + tm, :]
    u2 = ext_ref[SUBLANES - 2:SUBLANES - 2 + tm, :]
    wc = wc_ref[...]
    conv = wc[0:1] * u2 + wc[1:2] * u1 + wc[2:3] * u
    c = _rms(bg_ref[...] * conv, gc_ref[...]).astype(BF16)
    a = _rms(a_ref[...], ga_ref[...]).astype(BF16)
    return (x_ref[...] + _dot(a, wo_ref[0:SB_WIDTH, :])
            + _dot(c, wo_ref[SB_WIDTH:SB_WIDTH + CONV_CH, :]))


def _mixer_out_body(*refs, tiles_per_seq):
    *ins, h_ref, ext_ref = refs
    h_ref[...] = _mixer_math(*ins, ext_ref, tiles_per_seq)


def _mixer_specs(tm, tiles_per_seq, w_out_shape):
    per8 = tm // SUBLANES
    row = lambda width: pl.BlockSpec((tm, width), lambda i: (i, 0))
    return [
        row(D_MODEL), row(SB_WIDTH), row(CONV_CH), row(CONV_CH),
        pl.BlockSpec((SUBLANES, CONV_CH), lambda i: (jnp.maximum(i * per8 - 1, 0), 0)),
        pl.BlockSpec((1, SUBLANES, CONV_CH), lambda i: (i // tiles_per_seq, 0, 0)),
        _resident((CONV_K, CONV_CH)), _resident((1, SB_WIDTH)), _resident((1, CONV_CH)),
        _resident(w_out_shape),
    ]


def _mixer_out(x, attn, bg, u, init, w_conv, g_attn, g_conv, w_out, seq):
    n = x.shape[0]
    tm = min(ROW_TILE, seq)
    tiles_per_seq = seq // tm
    return pl.pallas_call(
        functools.partial(_mixer_out_body, tiles_per_seq=tiles_per_seq),
        grid=(n // tm,),
        in_specs=_mixer_specs(tm, tiles_per_seq, w_out.shape),
        out_specs=pl.BlockSpec((tm, D_MODEL), lambda i: (i, 0)),
        out_shape=jax.ShapeDtypeStruct((n, D_MODEL), F32),
        scratch_shapes=[pltpu.VMEM((tm + SUBLANES, CONV_CH), F32)],
        compiler_params=_cparams("parallel"),
        name="mixer_out",
    )(x, attn, bg, u, u, init, w_conv, g_attn, g_conv, w_out)


def _memory_kv_body(m_ref, g_ref, wk_ref, wv_ref, k_ref, v_ref):
    tm = m_ref.shape[0]
    mn = _rms(m_ref[...], g_ref[...]).astype(BF16)
    for w_ref, o_ref in ((wk_ref, k_ref), (wv_ref, v_ref)):
        proj = _dot(mn, w_ref[...])
        for head in range(CA_HEADS):
            o_ref[pl.ds(head, tm, stride=CA_HEADS), :] = (
                proj[:, head * CA_HEAD_DIM:(head + 1) * CA_HEAD_DIM])


def _memory_kv(mem, g, w_ck, w_cv):
    n = mem.shape[0]
    tm = min(ROW_TILE, n)
    row = lambda width: pl.BlockSpec((tm, width), lambda i: (i, 0))
    return pl.pallas_call(
        _memory_kv_body,
        grid=(n // tm,),
        in_specs=[row(D_MODEL), _resident((1, D_MODEL)), _resident(w_ck.shape), _resident(w_cv.shape)],
        out_specs=[pl.BlockSpec((tm * CA_HEADS, CA_HEAD_DIM), lambda i: (i, 0))] * 2,
        out_shape=[jax.ShapeDtypeStruct((n * CA_HEADS, CA_HEAD_DIM), F32)] * 2,
        compiler_params=_cparams("parallel"),
        name="memory_kv",
    )(mem, g, w_ck, w_cv)


def _cross_math(h, g_ref, wq_ref, mk_ref, mv_ref, wo_ref):
    n_b = mk_ref.shape[0] // (N_MEM * CA_HEADS)
    r = h.shape[0] // n_b
    q = _dot(_rms(h, g_ref[...]).astype(BF16), wq_ref[...])
    per_batch = []
    for b in range(n_b):
        outs = []
        for head in range(CA_HEADS):
            mem_rows = pl.ds(b * N_MEM * CA_HEADS + head, N_MEM, stride=CA_HEADS)
            qh = q[b * r:(b + 1) * r, head * CA_HEAD_DIM:(head + 1) * CA_HEAD_DIM].astype(BF16)
            sc = _dot_nt(qh, mk_ref[mem_rows, :].astype(BF16)) * CA_SCALE
            p = jnp.exp(sc - jnp.max(sc, axis=-1, keepdims=True))
            denom = jnp.sum(p, axis=-1, keepdims=True)
            outs.append(_dot(p.astype(BF16), mv_ref[mem_rows, :].astype(BF16)) / denom)
        per_batch.append(jnp.concatenate(outs, axis=-1))
    o = jnp.concatenate(per_batch, axis=0).astype(BF16)
    return h + _dot(o, wo_ref[...])


def _cross_body(h_ref, *refs):
    *ins, o_ref = refs
    o_ref[...] = _cross_math(h_ref[...], *ins)


def _cross(h, g, w_cq, mk, mv, w_co, rows_per_batch):
    n = h.shape[0]
    assert rows_per_batch < ROW_TILE
    n_b = min(CROSS_BATCHES_PER_TILE, n // rows_per_batch)
    tm = n_b * rows_per_batch
    row = pl.BlockSpec((tm, D_MODEL), lambda i: (i, 0))
    mem = pl.BlockSpec((n_b * N_MEM * CA_HEADS, CA_HEAD_DIM), lambda i: (i, 0))
    return pl.pallas_call(
        _cross_body,
        grid=(n // tm,),
        in_specs=[row, _resident((1, D_MODEL)), _resident(w_cq.shape), mem, mem,
                  _resident(w_co.shape)],
        out_specs=row,
        out_shape=jax.ShapeDtypeStruct((n, D_MODEL), F32),
        compiler_params=_cparams("parallel"),
        name="cross_attn",
    )(h, g, w_cq, mk, mv, w_co)


def _ffn_math(h, g_ref, wg_ref, wu_ref, wd_ref, gf_ref):
    hn = _rms(h, g_ref[...]).astype(BF16)
    acc = h
    for c in range(0, wg_ref.shape[1], FFN_CHUNK):
        gate = _dot(hn, wg_ref[:, c:c + FFN_CHUNK])
        up = _dot(hn, wu_ref[:, c:c + FFN_CHUNK])
        act = (gate / (1.0 + jnp.exp(-gate)) * up).astype(BF16)
        acc = acc + _dot(act, wd_ref[c:c + FFN_CHUNK, :])
    return _rms(acc, gf_ref[...])


def _ffn_body(h_ref, *refs):
    *ins, y_ref = refs
    y_ref[...] = _ffn_math(h_ref[...], *ins)


def _ffn(h, g, w_gate, w_up, w_down, g_final):
    n = h.shape[0]
    tm = min(ROW_TILE, n)
    row = pl.BlockSpec((tm, D_MODEL), lambda i: (i, 0))
    return pl.pallas_call(
        _ffn_body,
        grid=(n // tm,),
        in_specs=[row, _resident((1, D_MODEL)), _resident(w_gate.shape), _resident(w_up.shape),
                  _resident(w_down.shape), _resident((1, D_MODEL))],
        out_specs=row,
        out_shape=jax.ShapeDtypeStruct((n, D_MODEL), F32),
        compiler_params=_cparams("parallel"),
        name="ffn",
    )(h, g, w_gate, w_up, w_down, g_final)


N_MIXER_INS, N_CROSS_INS = 10, 5


def _tail_body(*refs, tiles_per_seq):
    *ins, y_ref, ext_ref = refs
    mixer_ins, ins = ins[:N_MIXER_INS], ins[N_MIXER_INS:]
    cross_ins, ffn_ins = ins[:N_CROSS_INS], ins[N_CROSS_INS:]
    h = _mixer_math(*mixer_ins, ext_ref, tiles_per_seq)
    h = _cross_math(h, *cross_ins)
    y_ref[...] = _ffn_math(h, *ffn_ins)


def _tail(x, attn, bg, u, init, w_conv, g_attn, g_conv, w_out, g_ca, w_cq, mk, mv, w_co,
          g_ffn, w_gate, w_up, w_down, g_final, seq):
    n = x.shape[0]
    tm = min(ROW_TILE, seq)
    tiles_per_seq = seq // tm
    vec_spec = _resident((1, D_MODEL))
    mem = pl.BlockSpec((N_MEM * CA_HEADS, CA_HEAD_DIM), lambda i: (i // tiles_per_seq, 0))
    return pl.pallas_call(
        functools.partial(_tail_body, tiles_per_seq=tiles_per_seq),
        grid=(n // tm,),
        in_specs=_mixer_specs(tm, tiles_per_seq, w_out.shape)
                 + [vec_spec, _resident(w_cq.shape), mem, mem, _resident(w_co.shape)]
                 + [vec_spec, _resident(w_gate.shape), _resident(w_up.shape),
                    _resident(w_down.shape), vec_spec],
        out_specs=pl.BlockSpec((tm, D_MODEL), lambda i: (i, 0)),
        out_shape=jax.ShapeDtypeStruct((n, D_MODEL), F32),
        scratch_shapes=[pltpu.VMEM((tm + SUBLANES, CONV_CH), F32)],
        compiler_params=_cparams("parallel"),
        name="tail",
    )(x, attn, bg, u, u, init, w_conv, g_attn, g_conv, w_out, g_ca, w_cq, mk, mv, w_co,
      g_ffn, w_gate, w_up, w_down, g_final)


def kernel(x_prompt, x_sample, mem_prompt, cache_sb_k, cache_sb_v, cache_mem_k, cache_mem_v,
           state_conv, page_table, g_mix, w_in, sb_bias, w_conv, g_attn_out, g_conv_out, w_out,
           g_mem, g_ca, w_cq, w_ck, w_cv, w_co, g_ffn, w_gate, w_up, w_down, g_final):
    depth = w_in.shape[0]
    assert depth == 1, "single-layer stack"
    batch, seq, _ = x_prompt.shape
    dec_batch, dec_seq, _ = x_sample.shape
    n_s = dec_batch * dec_seq
    assert seq % ROW_TILE == 0 and seq % IN_PROJ_TILE == 0
    assert IN_PROJ_TILE % SB_BLOCK == 0 and n_s % SB_BLOCK == 0
    assert dec_seq == SUBLANES
    l = 0
    bf = lambda w: w[l].astype(BF16)
    vec = lambda g: g[l][None, :]
    w_out_b = bf(w_out)
    w_in_b = bf(w_in)
    w_rows = jnp.concatenate([w_in_b[:, :SB_WIDTH], w_in_b[:, 3 * SB_WIDTH:]], axis=1)
    w_kv_t = w_in_b[:, SB_WIDTH:3 * SB_WIDTH].T
    w_cq_b, w_ck_b, w_cv_b, w_co_b = bf(w_cq), bf(w_ck), bf(w_cv), bf(w_co)
    w_gate_b, w_up_b, w_down_b = bf(w_gate), bf(w_up), bf(w_down)
    g_fin = g_final[None, :]
    heads = lambda t: t.reshape(1, t.shape[0], SB_HEADS, SB_HEAD_DIM, t.shape[2]).transpose(0, 1, 4, 2, 3)

    def tail(x2, attn, bg, u, init, mk, mv, seq_len, rows_per_batch):
        h = _mixer_out(x2, attn, bg, u, init, w_conv[l], vec(g_attn_out), vec(g_conv_out),
                       w_out_b, seq_len)
        h = _cross(h, vec(g_ca), w_cq_b, mk, mv, w_co_b, rows_per_batch)
        return _ffn(h, vec(g_ffn), w_gate_b, w_up_b, w_down_b, g_fin)

    xp = x_prompt.reshape(batch * seq, D_MODEL)
    xs = x_sample.reshape(n_s, D_MODEL)
    q, kt, vt, ktb, vtb, bg, u = _in_proj(xp, vec(g_mix), w_rows, w_kv_t, batch, seq)
    qs, kts, vts, _, _, bgs, us = _in_proj(xs, vec(g_mix), w_rows, w_kv_t, 1, n_s)
    ks, vs = kts[0].T, vts[0].T
    attn, attn_s = _sb_attention(q, ktb, vtb, sb_bias[l], batch, seq, qs.astype(F32), ks, vs,
                                 cache_sb_k[l], cache_sb_v[l], page_table)

    mk, mv = _memory_kv(mem_prompt.reshape(batch * N_MEM, D_MODEL), vec(g_mem), w_ck_b, w_cv_b)
    init_p = jnp.zeros((batch, SUBLANES, CONV_CH), F32)
    y_prompt = _tail(xp, attn, bg, u, init_p, w_conv[l], vec(g_attn_out), vec(g_conv_out), w_out_b,
                     vec(g_ca), w_cq_b, mk, mv, w_co_b,
                     vec(g_ffn), w_gate_b, w_up_b, w_down_b, g_fin, seq).reshape(batch, seq, D_MODEL)
    conv_p = u.reshape(batch, seq, CONV_CH)[:, seq - (CONV_K - 1):][None]
    mem_shape = (1, batch, N_MEM, CA_HEADS, CA_HEAD_DIM)

    init_s = jnp.pad(state_conv[l], ((0, 0), (SUBLANES - (CONV_K - 1), 0), (0, 0)))
    y_sample = tail(xs, attn_s, bgs, us, init_s,
                    cache_mem_k[l].reshape(dec_batch * N_MEM * CA_HEADS, CA_HEAD_DIM),
                    cache_mem_v[l].reshape(dec_batch * N_MEM * CA_HEADS, CA_HEAD_DIM),
                    dec_seq, dec_seq).reshape(dec_batch, dec_seq, D_MODEL)
    conv_s = us.reshape(dec_batch, dec_seq, CONV_CH)[:, dec_seq - (CONV_K - 1):][None]

    new_kv = lambda t: t.reshape(1, dec_batch, dec_seq, SB_HEADS, SB_HEAD_DIM)
    return (y_prompt, y_sample, heads(kt), heads(vt), conv_p,
            mk.reshape(mem_shape), mv.reshape(mem_shape), new_kv(ks), new_kv(vs), conv_s)
```

```python
import functools

import jax
import jax.numpy as jnp
from jax import lax
from jax.experimental import pallas as pl
from jax.experimental.pallas import tpu as pltpu

F32 = jnp.float32
BF16 = jnp.bfloat16

D_MODEL = 1024
SB_HEAD_DIM = 64
SB_WIDTH = 512
SB_HEADS = SB_WIDTH // SB_HEAD_DIM
SB_SCALE = SB_HEAD_DIM ** -0.5
CONV_CH = 512
CONV_K = 3
N_MEM = 256
CA_HEADS = 4
CA_HEAD_DIM = 128
CA_WIDTH = CA_HEADS * CA_HEAD_DIM
CA_SCALE = CA_HEAD_DIM ** -0.5
PAGE_SIZE = 128
EPS = 1e-6
LOG2E = 1.4426950408889634

LANES = 128
SUBLANES = 8
MXU_DIM = 256
VMEM_LIMIT_BYTES = 56 * 1024 * 1024

ROW_TILE = 512
IN_PROJ_TILE = 1024
SB_BLOCK = 256
FFN_CHUNK = 256
CROSS_BATCHES_PER_TILE = 8

SB_STAGE_ORDER = (
    ("logits", (0, 1)), ("log_terms", (0, 1)), ("logits", (2, 3)), ("suffixes", (0, 1)),
    ("log_terms", (2, 3)), ("weights", (0, 1)), ("suffixes", (2, 3)),
    ("values", (0, 1)), ("weights", (2, 3)), ("values", (2, 3)),
)


def _cparams(*sem):
    return pltpu.CompilerParams(dimension_semantics=sem, vmem_limit_bytes=VMEM_LIMIT_BYTES)


def _resident(shape):
    return pl.BlockSpec(shape, lambda *_: (0,) * len(shape), pipeline_mode=pl.Buffered(1))


def _rms(x, g):
    ms = jnp.mean(x * x, axis=-1, keepdims=True)
    return x * lax.rsqrt(ms + EPS) * g


def _dot(a, b):
    return jnp.dot(a, b, preferred_element_type=F32)


def _dot_nt(a, b):
    return lax.dot_general(a, b, (((1,), (1,)), ((), ())), preferred_element_type=F32)


def _log_sigmoids(z):
    soft = jnp.log(1.0 + jnp.exp2(jnp.abs(z) * -LOG2E))
    log_beta = jnp.minimum(z, 0.0) - soft
    return log_beta, log_beta - z


def _suffix_sums(log_om, tmat):
    return _dot(log_om.astype(BF16), tmat)


def _strict_lower_ones(n):
    row = lax.broadcasted_iota(jnp.int32, (n, n), 0)
    col = lax.broadcasted_iota(jnp.int32, (n, n), 1)
    return (row > col).astype(BF16)


def _in_proj_body(x_ref, g_ref, w_ref, wkv_ref, q_ref, kt_ref, vt_ref, ktb_ref, vtb_ref, bg_ref, u_ref):
    hn = _rms(x_ref[...], g_ref[...]).astype(BF16)

    def proj(c):
        return _dot(hn, w_ref[:, c * SB_WIDTH:(c + 1) * SB_WIDTH])

    q_ref[...] = (proj(0) * SB_SCALE).astype(BF16)
    bg_ref[...] = proj(1)
    u_ref[...] = proj(2) * proj(3)
    kv_t = _dot_nt(wkv_ref[...], hn)
    kt, vt = kv_t[:SB_WIDTH], kv_t[SB_WIDTH:]
    kt_ref[...] = kt
    vt_ref[...] = vt
    for c in range(ktb_ref.shape[0]):
        cols = slice(c * SB_BLOCK, (c + 1) * SB_BLOCK)
        ktb_ref[c] = kt[:, cols].astype(BF16)
        vtb_ref[c] = vt[:, cols].astype(BF16)


def _in_proj(x, g, w_rows, w_kv_t, batch, seq):
    n = batch * seq
    tm = min(IN_PROJ_TILE, seq)
    tiles = seq // tm
    blocks = tm // SB_BLOCK
    row = lambda width: pl.BlockSpec((tm, width), lambda b, i: (b * tiles + i, 0))
    t_spec = pl.BlockSpec((None, SB_WIDTH, tm), lambda b, i: (b, 0, i))
    tb_spec = pl.BlockSpec((None, blocks, SB_WIDTH, SB_BLOCK), lambda b, i: (b, i, 0, 0))
    rows = lambda dt: jax.ShapeDtypeStruct((n, SB_WIDTH), dt)
    t_shape = jax.ShapeDtypeStruct((batch, SB_WIDTH, seq), F32)
    tb_shape = jax.ShapeDtypeStruct((batch, seq // SB_BLOCK, SB_WIDTH, SB_BLOCK), BF16)
    return pl.pallas_call(
        _in_proj_body,
        grid=(batch, tiles),
        in_specs=[row(D_MODEL), _resident((1, D_MODEL)), _resident(w_rows.shape),
                  _resident(w_kv_t.shape)],
        out_specs=[row(SB_WIDTH), t_spec, t_spec, tb_spec, tb_spec, row(SB_WIDTH), row(SB_WIDTH)],
        out_shape=[rows(BF16), t_shape, t_shape, tb_shape, tb_shape, rows(F32), rows(F32)],
        compiler_params=_cparams("parallel", "parallel"),
        name="in_proj",
    )(x, g, w_rows, w_kv_t)


def _prompt_step(i, bias_ref, q_ref, k_ref, v_ref, o_ref, qm_ref, acc_ref, carry_ref):
    blk = SB_BLOCK
    n_pairs = SB_HEADS // 2
    row = lax.broadcasted_iota(jnp.int32, (2 * blk, blk), 0)
    col = lax.broadcasted_iota(jnp.int32, (2 * blk, blk), 1)
    readable = col < (row & (blk - 1))
    top = lax.broadcasted_iota(jnp.int32, (2 * blk, 1), 0) < blk
    tmat = _strict_lower_ones(blk)
    left = lax.broadcasted_iota(jnp.int32, (blk, LANES), 1) < SB_HEAD_DIM

    for pair in range(n_pairs):
        q2 = q_ref[:, pl.ds(pair * LANES, LANES)]
        zero = jnp.zeros_like(q2)
        qm_ref[pair] = jnp.concatenate(
            [jnp.where(left, q2, zero), jnp.where(left, zero, q2)], axis=0)

    def block(j, diagonal):
        kt_blk, vt_blk = k_ref[j], v_ref[j]
        carries = None if diagonal else [carry_ref[pair] for pair in range(n_pairs)]
        pairs = range(n_pairs)
        group = [slice(pair * LANES, (pair + 1) * LANES) for pair in pairs]
        qk, log_beta, log_om, suffix, a, outs, totals = ({} for _ in range(7))

        def logits(pair):
            qk[pair] = _dot(qm_ref[pair], kt_blk[group[pair]])

        def log_terms(pair):
            bias = jnp.where(top, bias_ref[2 * pair], bias_ref[2 * pair + 1])
            log_beta[pair], lo = _log_sigmoids(qk[pair] + bias)
            log_om[pair] = jnp.where(readable, lo, 0.0) if diagonal else lo
            totals[pair] = jnp.sum(log_om[pair], axis=1, keepdims=True)

        def suffixes(pair):
            suffix[pair] = _suffix_sums(log_om[pair], tmat)

        def weights(pair):
            if diagonal:
                a[pair] = jnp.where(readable, jnp.exp(log_beta[pair] + suffix[pair]), 0.0)
            else:
                a[pair] = jnp.exp(log_beta[pair] + suffix[pair] + carries[pair])

        def values(pair):
            outs[pair] = _dot_nt(a[pair].astype(BF16), vt_blk[group[pair]])

        for stage, members in SB_STAGE_ORDER:
            for pair in members:
                {"logits": logits, "log_terms": log_terms, "suffixes": suffixes,
                 "weights": weights, "values": values}[stage](pair)
        for pair in pairs:
            if diagonal:
                acc_ref[pair] = outs[pair]
                carry_ref[pair] = totals[pair]
            else:
                acc_ref[pair] += outs[pair]
                carry_ref[pair] = carries[pair] + totals[pair]

    block(i, True)

    @pl.loop(0, i)
    def _(jj):
        block(i - 1 - jj, False)

    for pair in range(n_pairs):
        o = acc_ref[pair]
        o_ref[:, pl.ds(pair * LANES, LANES)] = jnp.where(left, o[:blk], o[blk:])


def _sample_pages(g, q_ref, kn_ref, vn_ref, brow_ref, k_pages, v_pages, qbd_ref, carry_ref, acc_ref):
    t_new = q_ref.shape[0]
    rows = SB_HEADS * t_new
    tmat = _strict_lower_ones(PAGE_SIZE)
    row_head = lax.broadcasted_iota(jnp.int32, (rows, SB_WIDTH), 0) // t_new
    col_head = lax.broadcasted_iota(jnp.int32, (rows, SB_WIDTH), 1) // SB_HEAD_DIM

    @pl.when(g == 0)
    def _():
        q_rows = jnp.concatenate([q_ref[...]] * SB_HEADS, axis=0)
        qbd = jnp.where(row_head == col_head, q_rows, 0.0).astype(BF16)
        qbd_ref[...] = qbd
        pad = jnp.zeros((PAGE_SIZE - t_new, SB_WIDTH), F32)
        k_new = jnp.concatenate([kn_ref[...], pad], axis=0).astype(BF16)
        v_new = jnp.concatenate([vn_ref[...], pad], axis=0).astype(BF16)
        key = lax.broadcasted_iota(jnp.int32, (rows, PAGE_SIZE), 1)
        readable = key < lax.broadcasted_iota(jnp.int32, (rows, PAGE_SIZE), 0) % t_new
        log_beta, log_om = _log_sigmoids(_dot_nt(qbd, k_new) + brow_ref[:, :PAGE_SIZE])
        log_om = jnp.where(readable, log_om, 0.0)
        a = jnp.where(readable, jnp.exp(log_beta + _suffix_sums(log_om, tmat)), 0.0)
        acc_ref[...] = _dot(a.astype(BF16), v_new)
        carry_ref[...] = jnp.broadcast_to(jnp.sum(log_om, axis=1, keepdims=True), carry_ref.shape)

    kt = jnp.concatenate([k[...] for k in k_pages], axis=1).astype(BF16)
    vt = jnp.concatenate([v[...] for v in v_pages], axis=1).astype(BF16)
    log_beta, log_om = _log_sigmoids(_dot(qbd_ref[...], kt) + brow_ref[...])
    carry = carry_ref[...]
    a = []
    for s in range(len(k_pages)):
        cols = slice(s * PAGE_SIZE, (s + 1) * PAGE_SIZE)
        a.append(jnp.exp(log_beta[:, cols] + _suffix_sums(log_om[:, cols], tmat) + carry))
        carry = carry + jnp.sum(log_om[:, cols], axis=1, keepdims=True)
    carry_ref[...] = carry
    acc_ref[...] += _dot_nt(jnp.concatenate(a, axis=1).astype(BF16), vt)


def _sample_output(acc_ref, o_ref):
    t_new = o_ref.shape[0]
    acc = acc_ref[...]
    col_head = lax.broadcasted_iota(jnp.int32, (t_new, SB_WIDTH), 1) // SB_HEAD_DIM
    o = jnp.zeros((t_new, SB_WIDTH), F32)
    for head in range(SB_HEADS):
        o = o + jnp.where(col_head == head, acc[head * t_new:(head + 1) * t_new], 0.0)
    o_ref[...] = o


def _sb_body(pt_ref, bias_ref, q_ref, k_ref, v_ref, qs_ref, kn_ref, vn_ref, brow_ref, *refs,
             pps, n_groups):
    del pt_ref
    k_pages, v_pages = refs[:pps], refs[pps:2 * pps]
    o_ref, os_ref, qm_ref, acc_ref, carry_ref, qbd_ref, scarry_ref, sacc_ref = refs[2 * pps:]
    step = pl.program_id(0) * pl.num_programs(1) + pl.program_id(1)
    g = lax.rem(step, n_groups)
    _sample_pages(g, qs_ref, kn_ref, vn_ref, brow_ref, k_pages, v_pages, qbd_ref, scarry_ref, sacc_ref)
    _prompt_step(pl.program_id(1), bias_ref, q_ref, k_ref, v_ref, o_ref, qm_ref, acc_ref, carry_ref)

    @pl.when(g == n_groups - 1)
    def _():
        _sample_output(sacc_ref, os_ref)


def _sb_attention(q, ktb, vtb, bias, batch, seq, qs, ks_new, vs_new, cache_k, cache_v, page_table):
    nq = seq // SB_BLOCK
    n_pairs = SB_HEADS // 2
    dec_batch, n_pages = page_table.shape
    t_new = qs.shape[0] // dec_batch
    steps = batch * nq
    assert steps % dec_batch == 0 and (dec_batch * n_pages) % steps == 0
    n_groups = steps // dec_batch
    pps = n_pages // n_groups
    rows = SB_HEADS * t_new
    brow = jnp.broadcast_to(jnp.repeat(bias, t_new)[:, None], (rows, pps * PAGE_SIZE))
    keys_minor = lambda c: c.transpose(0, 2, 3, 1).reshape(c.shape[0], SB_WIDTH, PAGE_SIZE)
    seq_of = lambda b, i: lax.div(b * nq + i, n_groups)
    newest_first = page_table[:, ::-1].reshape(-1)

    def page_spec(slot):
        return pl.BlockSpec((None, SB_WIDTH, PAGE_SIZE),
                            lambda b, i, pt: (pt[(b * nq + i) * pps + slot], 0, 0))

    q_spec = pl.BlockSpec((SB_BLOCK, SB_WIDTH), lambda b, i, pt: (b * nq + i, 0))
    kv_spec = pl.BlockSpec((None, nq, SB_WIDTH, SB_BLOCK), lambda b, i, pt: (b, 0, 0, 0))
    tok_spec = pl.BlockSpec((t_new, SB_WIDTH), lambda b, i, pt: (seq_of(b, i), 0))
    grid_spec = pltpu.PrefetchScalarGridSpec(
        num_scalar_prefetch=1,
        grid=(batch, nq),
        in_specs=[pl.BlockSpec(memory_space=pltpu.SMEM), q_spec, kv_spec, kv_spec,
                  tok_spec, tok_spec, tok_spec, pl.BlockSpec(brow.shape, lambda b, i, pt: (0, 0))]
                 + [page_spec(slot) for slot in range(pps)] * 2,
        out_specs=[q_spec, tok_spec],
        scratch_shapes=[pltpu.VMEM((n_pairs, 2 * SB_BLOCK, LANES), BF16),
                        pltpu.VMEM((n_pairs, 2 * SB_BLOCK, LANES), F32),
                        pltpu.VMEM((n_pairs, 2 * SB_BLOCK, 1), F32),
                        pltpu.VMEM((rows, SB_WIDTH), BF16),
                        pltpu.VMEM((rows, PAGE_SIZE), F32),
                        pltpu.VMEM((rows, SB_WIDTH), F32)],
    )
    return pl.pallas_call(
        functools.partial(_sb_body, pps=pps, n_groups=n_groups),
        grid_spec=grid_spec,
        out_shape=[jax.ShapeDtypeStruct((batch * seq, SB_WIDTH), F32),
                   jax.ShapeDtypeStruct(qs.shape, F32)],
        compiler_params=_cparams("arbitrary", "arbitrary"),
        name="sb_attention",
    )(newest_first, bias, q, ktb, vtb, qs, ks_new, vs_new, brow,
      *([keys_minor(cache_k)] * pps), *([keys_minor(cache_v)] * pps))


def _mixer_math(x_ref, a_ref, bg_ref, u_ref, uprev_ref, init_ref, wc_ref, ga_ref, gc_ref,
                wo_ref, ext_ref, tiles_per_seq):
    tm = u_ref.shape[0]
    n_seq = init_ref.shape[0]
    rows = tm // n_seq
    i = pl.program_id(0)
    u = u_ref[...]
    u1, u2 = [], []
    for s in range(n_seq):
        if tiles_per_seq == 1:
            halo = init_ref[s]
        else:
            halo = jnp.where(lax.rem(i, tiles_per_seq) == 0, init_ref[0], uprev_ref[...])
        ext_ref[0:SUBLANES, :] = halo
        ext_ref[SUBLANES:SUBLANES + rows, :] = u[s * rows:(s + 1) * rows]
        u1.append(ext_ref[SUBLANES - 1:SUBLANES - 1 + rows, :])
        u2.append(ext_ref[SUBLANES - 2:SUBLANES - 2 + rows, :])
    u1, u2 = jnp.concatenate(u1, axis=0), jnp.concatenate(u2, axis=0)
    wc = wc_ref[...]
    conv = wc[0:1] * u2 + wc[1:2] * u1 + wc[2:3] * u
    c = _rms(bg_ref[...] * conv, gc_ref[...]).astype(BF16)
    a = _rms(a_ref[...], ga_ref[...]).astype(BF16)
    return (x_ref[...] + _dot(a, wo_ref[0:SB_WIDTH, :])
            + _dot(c, wo_ref[SB_WIDTH:SB_WIDTH + CONV_CH, :]))


def _mixer_out_body(*refs, tiles_per_seq):
    *ins, h_ref, ext_ref = refs
    h_ref[...] = _mixer_math(*ins, ext_ref, tiles_per_seq)


def _mixer_tiling(n, seq):
    if seq >= ROW_TILE:
        return ROW_TILE, seq // ROW_TILE, 1
    tm = min(ROW_TILE, n)
    return tm, 1, tm // seq


def _mixer_specs(tm, tiles_per_seq, seqs_per_tile, w_out_shape):
    per8 = tm // SUBLANES
    row = lambda width: pl.BlockSpec((tm, width), lambda i: (i, 0))
    return [
        row(D_MODEL), row(SB_WIDTH), row(CONV_CH), row(CONV_CH),
        pl.BlockSpec((SUBLANES, CONV_CH), lambda i: (jnp.maximum(i * per8 - 1, 0), 0)),
        pl.BlockSpec((seqs_per_tile, SUBLANES, CONV_CH), lambda i: (i // tiles_per_seq, 0, 0)),
        _resident((CONV_K, CONV_CH)), _resident((1, SB_WIDTH)), _resident((1, CONV_CH)),
        _resident(w_out_shape),
    ]


def _mixer_out(x, attn, bg, u, init, w_conv, g_attn, g_conv, w_out, seq):
    n = x.shape[0]
    tm, tiles_per_seq, seqs_per_tile = _mixer_tiling(n, seq)
    return pl.pallas_call(
        functools.partial(_mixer_out_body, tiles_per_seq=tiles_per_seq),
        grid=(n // tm,),
        in_specs=_mixer_specs(tm, tiles_per_seq, seqs_per_tile, w_out.shape),
        out_specs=pl.BlockSpec((tm, D_MODEL), lambda i: (i, 0)),
        out_shape=jax.ShapeDtypeStruct((n, D_MODEL), F32),
        scratch_shapes=[pltpu.VMEM((tm // seqs_per_tile + SUBLANES, CONV_CH), F32)],
        compiler_params=_cparams("parallel"),
        name="mixer_out",
    )(x, attn, bg, u, u, init, w_conv, g_attn, g_conv, w_out)


def _memory_kv_body(m_ref, g_ref, wk_ref, wv_ref, k_ref, v_ref):
    tm = m_ref.shape[0]
    mn = _rms(m_ref[...], g_ref[...]).astype(BF16)
    for w_ref, o_ref in ((wk_ref, k_ref), (wv_ref, v_ref)):
        proj = _dot(mn, w_ref[...])
        for head in range(CA_HEADS):
            o_ref[pl.ds(head, tm, stride=CA_HEADS), :] = (
                proj[:, head * CA_HEAD_DIM:(head + 1) * CA_HEAD_DIM])


def _memory_kv(mem, g, w_ck, w_cv):
    n = mem.shape[0]
    tm = min(ROW_TILE, n)
    row = lambda width: pl.BlockSpec((tm, width), lambda i: (i, 0))
    return pl.pallas_call(
        _memory_kv_body,
        grid=(n // tm,),
        in_specs=[row(D_MODEL), _resident((1, D_MODEL)), _resident(w_ck.shape), _resident(w_cv.shape)],
        out_specs=[pl.BlockSpec((tm * CA_HEADS, CA_HEAD_DIM), lambda i: (i, 0))] * 2,
        out_shape=[jax.ShapeDtypeStruct((n * CA_HEADS, CA_HEAD_DIM), F32)] * 2,
        compiler_params=_cparams("parallel"),
        name="memory_kv",
    )(mem, g, w_ck, w_cv)


def _cross_math(h, g_ref, wq_ref, mk_ref, mv_ref, wo_ref):
    n_b = mk_ref.shape[0] // (N_MEM * CA_HEADS)
    r = h.shape[0] // n_b
    q = _dot(_rms(h, g_ref[...]).astype(BF16), wq_ref[...])
    per_batch = []
    for b in range(n_b):
        outs = []
        for head in range(CA_HEADS):
            mem_rows = pl.ds(b * N_MEM * CA_HEADS + head, N_MEM, stride=CA_HEADS)
            qh = q[b * r:(b + 1) * r, head * CA_HEAD_DIM:(head + 1) * CA_HEAD_DIM].astype(BF16)
            sc = _dot_nt(qh, mk_ref[mem_rows, :].astype(BF16)) * CA_SCALE
            p = jnp.exp(sc - jnp.max(sc, axis=-1, keepdims=True))
            denom = jnp.sum(p, axis=-1, keepdims=True)
            outs.append(_dot(p.astype(BF16), mv_ref[mem_rows, :].astype(BF16)) / denom)
        per_batch.append(jnp.concatenate(outs, axis=-1))
    o = jnp.concatenate(per_batch, axis=0).astype(BF16)
    return h + _dot(o, wo_ref[...])


def _cross_body(h_ref, *refs):
    *ins, o_ref = refs
    o_ref[...] = _cross_math(h_ref[...], *ins)


def _cross(h, g, w_cq, mk, mv, w_co, rows_per_batch):
    n = h.shape[0]
    assert rows_per_batch < ROW_TILE
    n_b = min(CROSS_BATCHES_PER_TILE, n // rows_per_batch)
    tm = n_b * rows_per_batch
    row = pl.BlockSpec((tm, D_MODEL), lambda i: (i, 0))
    mem = pl.BlockSpec((n_b * N_MEM * CA_HEADS, CA_HEAD_DIM), lambda i: (i, 0))
    return pl.pallas_call(
        _cross_body,
        grid=(n // tm,),
        in_specs=[row, _resident((1, D_MODEL)), _resident(w_cq.shape), mem, mem,
                  _resident(w_co.shape)],
        out_specs=row,
        out_shape=jax.ShapeDtypeStruct((n, D_MODEL), F32),
        compiler_params=_cparams("parallel"),
        name="cross_attn",
    )(h, g, w_cq, mk, mv, w_co)


def _ffn_math(h, g_ref, wg_ref, wu_ref, wd_ref, gf_ref):
    hn = _rms(h, g_ref[...]).astype(BF16)
    acc = h
    for c in range(0, wg_ref.shape[1], FFN_CHUNK):
        gate = _dot(hn, wg_ref[:, c:c + FFN_CHUNK])
        up = _dot(hn, wu_ref[:, c:c + FFN_CHUNK])
        act = (gate / (1.0 + jnp.exp(-gate)) * up).astype(BF16)
        acc = acc + _dot(act, wd_ref[c:c + FFN_CHUNK, :])
    return _rms(acc, gf_ref[...])


def _ffn_body(h_ref, *refs):
    *ins, y_ref = refs
    y_ref[...] = _ffn_math(h_ref[...], *ins)


def _ffn(h, g, w_gate, w_up, w_down, g_final):
    n = h.shape[0]
    tm = min(ROW_TILE, n)
    row = pl.BlockSpec((tm, D_MODEL), lambda i: (i, 0))
    return pl.pallas_call(
        _ffn_body,
        grid=(n // tm,),
        in_specs=[row, _resident((1, D_MODEL)), _resident(w_gate.shape), _resident(w_up.shape),
                  _resident(w_down.shape), _resident((1, D_MODEL))],
        out_specs=row,
        out_shape=jax.ShapeDtypeStruct((n, D_MODEL), F32),
        compiler_params=_cparams("parallel"),
        name="ffn",
    )(h, g, w_gate, w_up, w_down, g_final)


N_MIXER_INS, N_CROSS_INS = 10, 5


def _tail_body(*refs, tiles_per_seq):
    *ins, y_ref, ext_ref = refs
    mixer_ins, ins = ins[:N_MIXER_INS], ins[N_MIXER_INS:]
    cross_ins, ffn_ins = ins[:N_CROSS_INS], ins[N_CROSS_INS:]
    h = _mixer_math(*mixer_ins, ext_ref, tiles_per_seq)
    h = _cross_math(h, *cross_ins)
    y_ref[...] = _ffn_math(h, *ffn_ins)


def _tail(x, attn, bg, u, init, w_conv, g_attn, g_conv, w_out, g_ca, w_cq, mk, mv, w_co,
          g_ffn, w_gate, w_up, w_down, g_final, seq):
    n = x.shape[0]
    assert seq >= ROW_TILE
    tm, tiles_per_seq, seqs_per_tile = _mixer_tiling(n, seq)
    vec_spec = _resident((1, D_MODEL))
    mem = pl.BlockSpec((N_MEM * CA_HEADS, CA_HEAD_DIM), lambda i: (i // tiles_per_seq, 0))
    return pl.pallas_call(
        functools.partial(_tail_body, tiles_per_seq=tiles_per_seq),
        grid=(n // tm,),
        in_specs=_mixer_specs(tm, tiles_per_seq, seqs_per_tile, w_out.shape)
                 + [vec_spec, _resident(w_cq.shape), mem, mem, _resident(w_co.shape)]
                 + [vec_spec, _resident(w_gate.shape), _resident(w_up.shape),
                    _resident(w_down.shape), vec_spec],
        out_specs=pl.BlockSpec((tm, D_MODEL), lambda i: (i, 0)),
        out_shape=jax.ShapeDtypeStruct((n, D_MODEL), F32),
        scratch_shapes=[pltpu.VMEM((tm + SUBLANES, CONV_CH), F32)],
        compiler_params=_cparams("parallel"),
        name="tail",
    )(x, attn, bg, u, u, init, w_conv, g_attn, g_conv, w_out, g_ca, w_cq, mk, mv, w_co,
      g_ffn, w_gate, w_up, w_down, g_final)


def kernel(x_prompt, x_sample, mem_prompt, cache_sb_k, cache_sb_v, cache_mem_k, cache_mem_v,
           state_conv, page_table, g_mix, w_in, sb_bias, w_conv, g_attn_out, g_conv_out, w_out,
           g_mem, g_ca, w_cq, w_ck, w_cv, w_co, g_ffn, w_gate, w_up, w_down, g_final):
    depth = w_in.shape[0]
    assert depth == 1, "single-layer stack"
    batch, seq, _ = x_prompt.shape
    dec_batch, dec_seq, _ = x_sample.shape
    n_s = dec_batch * dec_seq
    assert seq % ROW_TILE == 0 and seq % IN_PROJ_TILE == 0
    assert IN_PROJ_TILE % SB_BLOCK == 0 and n_s % SB_BLOCK == 0
    assert dec_seq == SUBLANES
    l = 0
    bf = lambda w: w[l].astype(BF16)
    vec = lambda g: g[l][None, :]
    w_out_b = bf(w_out)
    w_in_b = bf(w_in)
    w_rows = jnp.concatenate([w_in_b[:, :SB_WIDTH], w_in_b[:, 3 * SB_WIDTH:]], axis=1)
    w_kv_t = w_in_b[:, SB_WIDTH:3 * SB_WIDTH].T
    w_cq_b, w_ck_b, w_cv_b, w_co_b = bf(w_cq), bf(w_ck), bf(w_cv), bf(w_co)
    w_gate_b, w_up_b, w_down_b = bf(w_gate), bf(w_up), bf(w_down)
    g_fin = g_final[None, :]
    heads = lambda t: t.reshape(1, t.shape[0], SB_HEADS, SB_HEAD_DIM, t.shape[2]).transpose(0, 1, 4, 2, 3)

    def tail(x2, attn, bg, u, init, mk, mv, seq_len, rows_per_batch):
        h = _mixer_out(x2, attn, bg, u, init, w_conv[l], vec(g_attn_out), vec(g_conv_out),
                       w_out_b, seq_len)
        h = _cross(h, vec(g_ca), w_cq_b, mk, mv, w_co_b, rows_per_batch)
        return _ffn(h, vec(g_ffn), w_gate_b, w_up_b, w_down_b, g_fin)

    xp = x_prompt.reshape(batch * seq, D_MODEL)
    xs = x_sample.reshape(n_s, D_MODEL)
    q, kt, vt, ktb, vtb, bg, u = _in_proj(xp, vec(g_mix), w_rows, w_kv_t, batch, seq)
    qs, kts, vts, _, _, bgs, us = _in_proj(xs, vec(g_mix), w_rows, w_kv_t, 1, n_s)
    ks, vs = kts[0].T, vts[0].T
    attn, attn_s = _sb_attention(q, ktb, vtb, sb_bias[l], batch, seq, qs.astype(F32), ks, vs,
                                 cache_sb_k[l], cache_sb_v[l], page_table)

    mk, mv = _memory_kv(mem_prompt.reshape(batch * N_MEM, D_MODEL), vec(g_mem), w_ck_b, w_cv_b)
    init_p = jnp.zeros((batch, SUBLANES, CONV_CH), F32)
    y_prompt = _tail(xp, attn, bg, u, init_p, w_conv[l], vec(g_attn_out), vec(g_conv_out), w_out_b,
                     vec(g_ca), w_cq_b, mk, mv, w_co_b,
                     vec(g_ffn), w_gate_b, w_up_b, w_down_b, g_fin, seq).reshape(batch, seq, D_MODEL)
    conv_p = u.reshape(batch, seq, CONV_CH)[:, seq - (CONV_K - 1):][None]
    mem_shape = (1, batch, N_MEM, CA_HEADS, CA_HEAD_DIM)

    init_s = jnp.pad(state_conv[l], ((0, 0), (SUBLANES - (CONV_K - 1), 0), (0, 0)))
    y_sample = tail(xs, attn_s, bgs, us, init_s,
                    cache_mem_k[l].reshape(dec_batch * N_MEM * CA_HEADS, CA_HEAD_DIM),
                    cache_mem_v[l].reshape(dec_batch * N_MEM * CA_HEADS, CA_HEAD_DIM),
                    dec_seq, dec_seq).reshape(dec_batch, dec_seq, D_MODEL)
    conv_s = us.reshape(dec_batch, dec_seq, CONV_CH)[:, dec_seq - (CONV_K - 1):][None]

    new_kv = lambda t: t.reshape(1, dec_batch, dec_seq, SB_HEADS, SB_HEAD_DIM)
    return (y_prompt, y_sample, heads(kt), heads(vt), conv_p,
            mk.reshape(mem_shape), mv.reshape(mem_shape), new_kv(ks), new_kv(vs), conv_s)
```

```python
import functools

import jax
import jax.numpy as jnp
from jax import lax
from jax.experimental import pallas as pl
from jax.experimental.pallas import tpu as pltpu

F32 = jnp.float32
BF16 = jnp.bfloat16

D_MODEL = 1024
SB_HEAD_DIM = 64
SB_WIDTH = 512
SB_HEADS = SB_WIDTH // SB_HEAD_DIM
SB_SCALE = SB_HEAD_DIM ** -0.5
CONV_CH = 512
CONV_K = 3
N_MEM = 256
CA_HEADS = 4
CA_HEAD_DIM = 128
CA_WIDTH = CA_HEADS * CA_HEAD_DIM
CA_SCALE = CA_HEAD_DIM ** -0.5
PAGE_SIZE = 128
EPS = 1e-6
LOG2E = 1.4426950408889634

LANES = 128
SUBLANES = 8
MXU_DIM = 256
VMEM_LIMIT_BYTES = 56 * 1024 * 1024

ROW_TILE = 512
IN_PROJ_TILE = 1024
SB_BLOCK = 256
FFN_CHUNK = 256
CROSS_BATCHES_PER_TILE = 8

SB_STAGE_ORDER = (
    ("logits", (0, 1)), ("log_terms", (0, 1)), ("logits", (2, 3)), ("suffixes", (0, 1)),
    ("log_terms", (2, 3)), ("weights", (0, 1)), ("suffixes", (2, 3)),
    ("values", (0, 1)), ("weights", (2, 3)), ("values", (2, 3)),
)


def _cparams(*sem):
    return pltpu.CompilerParams(dimension_semantics=sem, vmem_limit_bytes=VMEM_LIMIT_BYTES)


def _resident(shape):
    return pl.BlockSpec(shape, lambda *_: (0,) * len(shape), pipeline_mode=pl.Buffered(1))


def _rms(x, g):
    ms = jnp.mean(x * x, axis=-1, keepdims=True)
    return x * lax.rsqrt(ms + EPS) * g


def _dot(a, b):
    return jnp.dot(a, b, preferred_element_type=F32)


def _dot_nt(a, b):
    return lax.dot_general(a, b, (((1,), (1,)), ((), ())), preferred_element_type=F32)


def _log_sigmoids(z):
    soft = jnp.log(1.0 + jnp.exp2(jnp.abs(z) * -LOG2E))
    log_beta = jnp.minimum(z, 0.0) - soft
    return log_beta, log_beta - z


def _suffix_sums(log_om, tmat):
    return _dot(log_om.astype(BF16), tmat)


def _strict_lower_ones(n):
    row = lax.broadcasted_iota(jnp.int32, (n, n), 0)
    col = lax.broadcasted_iota(jnp.int32, (n, n), 1)
    return (row > col).astype(BF16)


def _in_proj_body(x_ref, g_ref, w_ref, wkv_ref, q_ref, kt_ref, vt_ref, ktb_ref, vtb_ref, bg_ref, u_ref):
    hn = _rms(x_ref[...], g_ref[...]).astype(BF16)

    def proj(c):
        return _dot(hn, w_ref[:, c * SB_WIDTH:(c + 1) * SB_WIDTH])

    q_ref[...] = (proj(0) * SB_SCALE).astype(BF16)
    bg_ref[...] = proj(1)
    u_ref[...] = proj(2) * proj(3)
    kv_t = _dot_nt(wkv_ref[...], hn)
    kt, vt = kv_t[:SB_WIDTH], kv_t[SB_WIDTH:]
    kt_ref[...] = kt
    vt_ref[...] = vt
    for c in range(ktb_ref.shape[0]):
        cols = slice(c * SB_BLOCK, (c + 1) * SB_BLOCK)
        ktb_ref[c] = kt[:, cols].astype(BF16)
        vtb_ref[c] = vt[:, cols].astype(BF16)


def _in_proj(x, g, w_rows, w_kv_t, batch, seq):
    n = batch * seq
    tm = min(IN_PROJ_TILE, seq)
    tiles = seq // tm
    blocks = tm // SB_BLOCK
    row = lambda width: pl.BlockSpec((tm, width), lambda b, i: (b * tiles + i, 0))
    t_spec = pl.BlockSpec((None, SB_WIDTH, tm), lambda b, i: (b, 0, i))
    tb_spec = pl.BlockSpec((None, blocks, SB_WIDTH, SB_BLOCK), lambda b, i: (b, i, 0, 0))
    rows = lambda dt: jax.ShapeDtypeStruct((n, SB_WIDTH), dt)
    t_shape = jax.ShapeDtypeStruct((batch, SB_WIDTH, seq), F32)
    tb_shape = jax.ShapeDtypeStruct((batch, seq // SB_BLOCK, SB_WIDTH, SB_BLOCK), BF16)
    return pl.pallas_call(
        _in_proj_body,
        grid=(batch, tiles),
        in_specs=[row(D_MODEL), _resident((1, D_MODEL)), _resident(w_rows.shape),
                  _resident(w_kv_t.shape)],
        out_specs=[row(SB_WIDTH), t_spec, t_spec, tb_spec, tb_spec, row(SB_WIDTH), row(SB_WIDTH)],
        out_shape=[rows(BF16), t_shape, t_shape, tb_shape, tb_shape, rows(F32), rows(F32)],
        compiler_params=_cparams("parallel", "parallel"),
        name="in_proj",
    )(x, g, w_rows, w_kv_t)


def _prompt_step(i, bias_ref, q_ref, k_ref, v_ref, o_ref, qm_ref, acc_ref, carry_ref):
    blk = SB_BLOCK
    n_pairs = SB_HEADS // 2
    row = lax.broadcasted_iota(jnp.int32, (2 * blk, blk), 0)
    col = lax.broadcasted_iota(jnp.int32, (2 * blk, blk), 1)
    readable = col < (row & (blk - 1))
    top = lax.broadcasted_iota(jnp.int32, (2 * blk, 1), 0) < blk
    tmat = _strict_lower_ones(blk)
    left = lax.broadcasted_iota(jnp.int32, (blk, LANES), 1) < SB_HEAD_DIM

    for pair in range(n_pairs):
        q2 = q_ref[:, pl.ds(pair * LANES, LANES)]
        zero = jnp.zeros_like(q2)
        qm_ref[pair] = jnp.concatenate(
            [jnp.where(left, q2, zero), jnp.where(left, zero, q2)], axis=0)

    def block(j, diagonal):
        kt_blk, vt_blk = k_ref[j], v_ref[j]
        carries = None if diagonal else [carry_ref[pair] for pair in range(n_pairs)]
        pairs = range(n_pairs)
        group = [slice(pair * LANES, (pair + 1) * LANES) for pair in pairs]
        qk, log_beta, log_om, suffix, a, outs, totals = ({} for _ in range(7))

        def logits(pair):
            qk[pair] = _dot(qm_ref[pair], kt_blk[group[pair]])

        def log_terms(pair):
            bias = jnp.where(top, bias_ref[2 * pair], bias_ref[2 * pair + 1])
            log_beta[pair], lo = _log_sigmoids(qk[pair] + bias)
            log_om[pair] = jnp.where(readable, lo, 0.0) if diagonal else lo
            totals[pair] = jnp.sum(log_om[pair], axis=1, keepdims=True)

        def suffixes(pair):
            suffix[pair] = _suffix_sums(log_om[pair], tmat)

        def weights(pair):
            if diagonal:
                a[pair] = jnp.where(readable, jnp.exp(log_beta[pair] + suffix[pair]), 0.0)
            else:
                a[pair] = jnp.exp(log_beta[pair] + suffix[pair] + carries[pair])

        def values(pair):
            outs[pair] = _dot_nt(a[pair].astype(BF16), vt_blk[group[pair]])

        for stage, members in SB_STAGE_ORDER:
            for pair in members:
                {"logits": logits, "log_terms": log_terms, "suffixes": suffixes,
                 "weights": weights, "values": values}[stage](pair)
        for pair in pairs:
            if diagonal:
                acc_ref[pair] = outs[pair]
                carry_ref[pair] = totals[pair]
            else:
                acc_ref[pair] += outs[pair]
                carry_ref[pair] = carries[pair] + totals[pair]

    block(i, True)

    @pl.loop(0, i)
    def _(jj):
        block(i - 1 - jj, False)

    for pair in range(n_pairs):
        o = acc_ref[pair]
        o_ref[:, pl.ds(pair * LANES, LANES)] = jnp.where(left, o[:blk], o[blk:])


def _sample_pages(g, q_ref, kn_ref, vn_ref, brow_ref, k_pages, v_pages, qbd_ref, carry_ref, acc_ref):
    t_new = q_ref.shape[0]
    rows = SB_HEADS * t_new
    tmat = _strict_lower_ones(PAGE_SIZE)
    row_head = lax.broadcasted_iota(jnp.int32, (rows, SB_WIDTH), 0) // t_new
    col_head = lax.broadcasted_iota(jnp.int32, (rows, SB_WIDTH), 1) // SB_HEAD_DIM

    @pl.when(g == 0)
    def _():
        q_rows = jnp.concatenate([q_ref[...]] * SB_HEADS, axis=0)
        qbd = jnp.where(row_head == col_head, q_rows, 0.0).astype(BF16)
        qbd_ref[...] = qbd
        pad = jnp.zeros((PAGE_SIZE - t_new, SB_WIDTH), F32)
        k_new = jnp.concatenate([kn_ref[...], pad], axis=0).astype(BF16)
        v_new = jnp.concatenate([vn_ref[...], pad], axis=0).astype(BF16)
        key = lax.broadcasted_iota(jnp.int32, (rows, PAGE_SIZE), 1)
        readable = key < lax.broadcasted_iota(jnp.int32, (rows, PAGE_SIZE), 0) % t_new
        log_beta, log_om = _log_sigmoids(_dot_nt(qbd, k_new) + brow_ref[:, :PAGE_SIZE])
        log_om = jnp.where(readable, log_om, 0.0)
        a = jnp.where(readable, jnp.exp(log_beta + _suffix_sums(log_om, tmat)), 0.0)
        acc_ref[...] = _dot(a.astype(BF16), v_new)
        carry_ref[...] = jnp.broadcast_to(jnp.sum(log_om, axis=1, keepdims=True), carry_ref.shape)

    kt = jnp.concatenate([k[...] for k in k_pages], axis=1).astype(BF16)
    vt = jnp.concatenate([v[...] for v in v_pages], axis=1).astype(BF16)
    log_beta, log_om = _log_sigmoids(_dot(qbd_ref[...], kt) + brow_ref[...])
    carry = carry_ref[...]
    a = []
    for s in range(len(k_pages)):
        cols = slice(s * PAGE_SIZE, (s + 1) * PAGE_SIZE)
        a.append(jnp.exp(log_beta[:, cols] + _suffix_sums(log_om[:, cols], tmat) + carry))
        carry = carry + jnp.sum(log_om[:, cols], axis=1, keepdims=True)
    carry_ref[...] = carry
    acc_ref[...] += _dot_nt(jnp.concatenate(a, axis=1).astype(BF16), vt)


def _sample_output(acc_ref, o_ref):
    t_new = o_ref.shape[0]
    acc = acc_ref[...]
    col_head = lax.broadcasted_iota(jnp.int32, (t_new, SB_WIDTH), 1) // SB_HEAD_DIM
    o = jnp.zeros((t_new, SB_WIDTH), F32)
    for head in range(SB_HEADS):
        o = o + jnp.where(col_head == head, acc[head * t_new:(head + 1) * t_new], 0.0)
    o_ref[...] = o


def _sb_body(pt_ref, bias_ref, q_ref, k_ref, v_ref, qs_ref, kn_ref, vn_ref, brow_ref, *refs,
             pps, n_groups):
    del pt_ref
    k_pages, v_pages = refs[:pps], refs[pps:2 * pps]
    o_ref, os_ref, qm_ref, acc_ref, carry_ref, qbd_ref, scarry_ref, sacc_ref = refs[2 * pps:]
    step = pl.program_id(0) * pl.num_programs(1) + pl.program_id(1)
    g = lax.rem(step, n_groups)
    _sample_pages(g, qs_ref, kn_ref, vn_ref, brow_ref, k_pages, v_pages, qbd_ref, scarry_ref, sacc_ref)
    _prompt_step(pl.program_id(1), bias_ref, q_ref, k_ref, v_ref, o_ref, qm_ref, acc_ref, carry_ref)

    @pl.when(g == n_groups - 1)
    def _():
        _sample_output(sacc_ref, os_ref)


def _sb_attention(q, ktb, vtb, bias, batch, seq, qs, ks_new, vs_new, cache_k, cache_v, page_table):
    nq = seq // SB_BLOCK
    n_pairs = SB_HEADS // 2
    dec_batch, n_pages = page_table.shape
    t_new = qs.shape[0] // dec_batch
    steps = batch * nq
    assert steps % dec_batch == 0 and (dec_batch * n_pages) % steps == 0
    n_groups = steps // dec_batch
    pps = n_pages // n_groups
    rows = SB_HEADS * t_new
    brow = jnp.broadcast_to(jnp.repeat(bias, t_new)[:, None], (rows, pps * PAGE_SIZE))
    keys_minor = lambda c: c.transpose(0, 2, 3, 1).reshape(c.shape[0], SB_WIDTH, PAGE_SIZE)
    seq_of = lambda b, i: lax.div(b * nq + i, n_groups)
    newest_first = page_table[:, ::-1].reshape(-1)

    def page_spec(slot):
        return pl.BlockSpec((None, SB_WIDTH, PAGE_SIZE),
                            lambda b, i, pt: (pt[(b * nq + i) * pps + slot], 0, 0))

    q_spec = pl.BlockSpec((SB_BLOCK, SB_WIDTH), lambda b, i, pt: (b * nq + i, 0))
    kv_spec = pl.BlockSpec((None, nq, SB_WIDTH, SB_BLOCK), lambda b, i, pt: (b, 0, 0, 0))
    tok_spec = pl.BlockSpec((t_new, SB_WIDTH), lambda b, i, pt: (seq_of(b, i), 0))
    grid_spec = pltpu.PrefetchScalarGridSpec(
        num_scalar_prefetch=1,
        grid=(batch, nq),
        in_specs=[pl.BlockSpec(memory_space=pltpu.SMEM), q_spec, kv_spec, kv_spec,
                  tok_spec, tok_spec, tok_spec, pl.BlockSpec(brow.shape, lambda b, i, pt: (0, 0))]
                 + [page_spec(slot) for slot in range(pps)] * 2,
        out_specs=[q_spec, tok_spec],
        scratch_shapes=[pltpu.VMEM((n_pairs, 2 * SB_BLOCK, LANES), BF16),
                        pltpu.VMEM((n_pairs, 2 * SB_BLOCK, LANES), F32),
                        pltpu.VMEM((n_pairs, 2 * SB_BLOCK, 1), F32),
                        pltpu.VMEM((rows, SB_WIDTH), BF16),
                        pltpu.VMEM((rows, PAGE_SIZE), F32),
                        pltpu.VMEM((rows, SB_WIDTH), F32)],
    )
    return pl.pallas_call(
        functools.partial(_sb_body, pps=pps, n_groups=n_groups),
        grid_spec=grid_spec,
        out_shape=[jax.ShapeDtypeStruct((batch * seq, SB_WIDTH), F32),
                   jax.ShapeDtypeStruct(qs.shape, F32)],
        compiler_params=_cparams("arbitrary", "arbitrary"),
        name="sb_attention",
    )(newest_first, bias, q, ktb, vtb, qs, ks_new, vs_new, brow,
      *([keys_minor(cache_k)] * pps), *([keys_minor(cache_v)] * pps))


def _mixer_math(x_ref, a_ref, bg_ref, u_ref, uprev_ref, init_ref, wc_ref, ga_ref, gc_ref,
                wo_ref, ext_ref, tiles_per_seq):
    tm = u_ref.shape[0]
    n_seq = init_ref.shape[0]
    rows = tm // n_seq
    i = pl.program_id(0)
    u = u_ref[...]
    u1, u2 = [], []
    for s in range(n_seq):
        if tiles_per_seq == 1:
            halo = init_ref[s]
        else:
            halo = jnp.where(lax.rem(i, tiles_per_seq) == 0, init_ref[0], uprev_ref[...])
        ext_ref[0:SUBLANES, :] = halo
        ext_ref[SUBLANES:SUBLANES + rows, :] = u[s * rows:(s + 1) * rows]
        u1.append(ext_ref[SUBLANES - 1:SUBLANES - 1 + rows, :])
        u2.append(ext_ref[SUBLANES - 2:SUBLANES - 2 + rows, :])
    u1, u2 = jnp.concatenate(u1, axis=0), jnp.concatenate(u2, axis=0)
    wc = wc_ref[...]
    conv = wc[0:1] * u2 + wc[1:2] * u1 + wc[2:3] * u
    c = _rms(bg_ref[...] * conv, gc_ref[...]).astype(BF16)
    a = _rms(a_ref[...], ga_ref[...]).astype(BF16)
    return (x_ref[...] + _dot(a, wo_ref[0:SB_WIDTH, :])
            + _dot(c, wo_ref[SB_WIDTH:SB_WIDTH + CONV_CH, :]))


def _mixer_out_body(*refs, tiles_per_seq):
    *ins, h_ref, ext_ref = refs
    h_ref[...] = _mixer_math(*ins, ext_ref, tiles_per_seq)


def _mixer_tiling(n, seq):
    if seq >= ROW_TILE:
        return ROW_TILE, seq // ROW_TILE, 1
    tm = min(ROW_TILE, n)
    return tm, 1, tm // seq


def _mixer_specs(tm, tiles_per_seq, seqs_per_tile, w_out_shape):
    per8 = tm // SUBLANES
    row = lambda width: pl.BlockSpec((tm, width), lambda i: (i, 0))
    return [
        row(D_MODEL), row(SB_WIDTH), row(CONV_CH), row(CONV_CH),
        pl.BlockSpec((SUBLANES, CONV_CH), lambda i: (jnp.maximum(i * per8 - 1, 0), 0)),
        pl.BlockSpec((seqs_per_tile, SUBLANES, CONV_CH), lambda i: (i // tiles_per_seq, 0, 0)),
        _resident((CONV_K, CONV_CH)), _resident((1, SB_WIDTH)), _resident((1, CONV_CH)),
        _resident(w_out_shape),
    ]


def _mixer_out(x, attn, bg, u, init, w_conv, g_attn, g_conv, w_out, seq):
    n = x.shape[0]
    tm, tiles_per_seq, seqs_per_tile = _mixer_tiling(n, seq)
    return pl.pallas_call(
        functools.partial(_mixer_out_body, tiles_per_seq=tiles_per_seq),
        grid=(n // tm,),
        in_specs=_mixer_specs(tm, tiles_per_seq, seqs_per_tile, w_out.shape),
        out_specs=pl.BlockSpec((tm, D_MODEL), lambda i: (i, 0)),
        out_shape=jax.ShapeDtypeStruct((n, D_MODEL), F32),
        scratch_shapes=[pltpu.VMEM((tm // seqs_per_tile + SUBLANES, CONV_CH), F32)],
        compiler_params=_cparams("parallel"),
        name="mixer_out",
    )(x, attn, bg, u, u, init, w_conv, g_attn, g_conv, w_out)


def _memory_kv_body(m_ref, g_ref, wk_ref, wv_ref, k_ref, v_ref):
    tm = m_ref.shape[0]
    mn = _rms(m_ref[...], g_ref[...]).astype(BF16)
    for w_ref, o_ref in ((wk_ref, k_ref), (wv_ref, v_ref)):
        proj = _dot(mn, w_ref[...])
        for head in range(CA_HEADS):
            o_ref[pl.ds(head, tm, stride=CA_HEADS), :] = (
                proj[:, head * CA_HEAD_DIM:(head + 1) * CA_HEAD_DIM])


def _memory_kv(mem, g, w_ck, w_cv):
    n = mem.shape[0]
    tm = min(ROW_TILE, n)
    row = lambda width: pl.BlockSpec((tm, width), lambda i: (i, 0))
    return pl.pallas_call(
        _memory_kv_body,
        grid=(n // tm,),
        in_specs=[row(D_MODEL), _resident((1, D_MODEL)), _resident(w_ck.shape), _resident(w_cv.shape)],
        out_specs=[pl.BlockSpec((tm * CA_HEADS, CA_HEAD_DIM), lambda i: (i, 0))] * 2,
        out_shape=[jax.ShapeDtypeStruct((n * CA_HEADS, CA_HEAD_DIM), F32)] * 2,
        compiler_params=_cparams("parallel"),
        name="memory_kv",
    )(mem, g, w_ck, w_cv)


def _cross_math(h, g_ref, wq_ref, mk_ref, mv_ref, wo_ref):
    n_b = mk_ref.shape[0] // (N_MEM * CA_HEADS)
    r = h.shape[0] // n_b
    q = _dot(_rms(h, g_ref[...]).astype(BF16), wq_ref[...])
    cells = [(b, head) for b in range(n_b) for head in range(CA_HEADS)]
    mem_rows = lambda b, head: pl.ds(b * N_MEM * CA_HEADS + head, N_MEM, stride=CA_HEADS)
    scores = [
        _dot_nt(q[b * r:(b + 1) * r, head * CA_HEAD_DIM:(head + 1) * CA_HEAD_DIM].astype(BF16),
                mk_ref[mem_rows(b, head), :].astype(BF16)) * CA_SCALE
        for b, head in cells]
    probs = [jnp.exp(sc - jnp.max(sc, axis=-1, keepdims=True)) for sc in scores]
    outs = [_dot(p.astype(BF16), mv_ref[mem_rows(b, head), :].astype(BF16))
            / jnp.sum(p, axis=-1, keepdims=True)
            for p, (b, head) in zip(probs, cells)]
    o = jnp.concatenate(
        [jnp.concatenate(outs[b * CA_HEADS:(b + 1) * CA_HEADS], axis=-1) for b in range(n_b)],
        axis=0).astype(BF16)
    return h + _dot(o, wo_ref[...])


def _cross_body(h_ref, *refs):
    *ins, o_ref = refs
    o_ref[...] = _cross_math(h_ref[...], *ins)


def _cross(h, g, w_cq, mk, mv, w_co, rows_per_batch):
    n = h.shape[0]
    assert rows_per_batch < ROW_TILE
    n_b = min(CROSS_BATCHES_PER_TILE, n // rows_per_batch)
    tm = n_b * rows_per_batch
    row = pl.BlockSpec((tm, D_MODEL), lambda i: (i, 0))
    mem = pl.BlockSpec((n_b * N_MEM * CA_HEADS, CA_HEAD_DIM), lambda i: (i, 0))
    return pl.pallas_call(
        _cross_body,
        grid=(n // tm,),
        in_specs=[row, _resident((1, D_MODEL)), _resident(w_cq.shape), mem, mem,
                  _resident(w_co.shape)],
        out_specs=row,
        out_shape=jax.ShapeDtypeStruct((n, D_MODEL), F32),
        compiler_params=_cparams("parallel"),
        name="cross_attn",
    )(h, g, w_cq, mk, mv, w_co)


def _ffn_math(h, g_ref, wg_ref, wu_ref, wd_ref, gf_ref):
    hn = _rms(h, g_ref[...]).astype(BF16)
    acc = h
    for c in range(0, wg_ref.shape[1], FFN_CHUNK):
        gate = _dot(hn, wg_ref[:, c:c + FFN_CHUNK])
        up = _dot(hn, wu_ref[:, c:c + FFN_CHUNK])
        act = (gate / (1.0 + jnp.exp(-gate)) * up).astype(BF16)
        acc = acc + _dot(act, wd_ref[c:c + FFN_CHUNK, :])
    return _rms(acc, gf_ref[...])


def _ffn_body(h_ref, *refs):
    *ins, y_ref = refs
    y_ref[...] = _ffn_math(h_ref[...], *ins)


def _ffn(h, g, w_gate, w_up, w_down, g_final):
    n = h.shape[0]
    tm = min(ROW_TILE, n)
    row = pl.BlockSpec((tm, D_MODEL), lambda i: (i, 0))
    return pl.pallas_call(
        _ffn_body,
        grid=(n // tm,),
        in_specs=[row, _resident((1, D_MODEL)), _resident(w_gate.shape), _resident(w_up.shape),
                  _resident(w_down.shape), _resident((1, D_MODEL))],
        out_specs=row,
        out_shape=jax.ShapeDtypeStruct((n, D_MODEL), F32),
        compiler_params=_cparams("parallel"),
        name="ffn",
    )(h, g, w_gate, w_up, w_down, g_final)


N_MIXER_INS, N_CROSS_INS = 10, 5


def _tail_body(*refs, tiles_per_seq):
    *ins, y_ref, ext_ref = refs
    mixer_ins, ins = ins[:N_MIXER_INS], ins[N_MIXER_INS:]
    cross_ins, ffn_ins = ins[:N_CROSS_INS], ins[N_CROSS_INS:]
    h = _mixer_math(*mixer_ins, ext_ref, tiles_per_seq)
    h = _cross_math(h, *cross_ins)
    y_ref[...] = _ffn_math(h, *ffn_ins)


def _tail(x, attn, bg, u, init, w_conv, g_attn, g_conv, w_out, g_ca, w_cq, mk, mv, w_co,
          g_ffn, w_gate, w_up, w_down, g_final, seq):
    n = x.shape[0]
    assert seq >= ROW_TILE
    tm, tiles_per_seq, seqs_per_tile = _mixer_tiling(n, seq)
    vec_spec = _resident((1, D_MODEL))
    mem = pl.BlockSpec((N_MEM * CA_HEADS, CA_HEAD_DIM), lambda i: (i // tiles_per_seq, 0))
    return pl.pallas_call(
        functools.partial(_tail_body, tiles_per_seq=tiles_per_seq),
        grid=(n // tm,),
        in_specs=_mixer_specs(tm, tiles_per_seq, seqs_per_tile, w_out.shape)
                 + [vec_spec, _resident(w_cq.shape), mem, mem, _resident(w_co.shape)]
                 + [vec_spec, _resident(w_gate.shape), _resident(w_up.shape),
                    _resident(w_down.shape), vec_spec],
        out_specs=pl.BlockSpec((tm, D_MODEL), lambda i: (i, 0)),
        out_shape=jax.ShapeDtypeStruct((n, D_MODEL), F32),
        scratch_shapes=[pltpu.VMEM((tm + SUBLANES, CONV_CH), F32)],
        compiler_params=_cparams("parallel"),
        name="tail",
    )(x, attn, bg, u, u, init, w_conv, g_attn, g_conv, w_out, g_ca, w_cq, mk, mv, w_co,
      g_ffn, w_gate, w_up, w_down, g_final)


def kernel(x_prompt, x_sample, mem_prompt, cache_sb_k, cache_sb_v, cache_mem_k, cache_mem_v,
           state_conv, page_table, g_mix, w_in, sb_bias, w_conv, g_attn_out, g_conv_out, w_out,
           g_mem, g_ca, w_cq, w_ck, w_cv, w_co, g_ffn, w_gate, w_up, w_down, g_final):
    depth = w_in.shape[0]
    assert depth == 1, "single-layer stack"
    batch, seq, _ = x_prompt.shape
    dec_batch, dec_seq, _ = x_sample.shape
    n_s = dec_batch * dec_seq
    assert seq % ROW_TILE == 0 and seq % IN_PROJ_TILE == 0
    assert IN_PROJ_TILE % SB_BLOCK == 0 and n_s % SB_BLOCK == 0
    assert dec_seq == SUBLANES
    l = 0
    bf = lambda w: w[l].astype(BF16)
    vec = lambda g: g[l][None, :]
    w_out_b = bf(w_out)
    w_in_b = bf(w_in)
    w_rows = jnp.concatenate([w_in_b[:, :SB_WIDTH], w_in_b[:, 3 * SB_WIDTH:]], axis=1)
    w_kv_t = w_in_b[:, SB_WIDTH:3 * SB_WIDTH].T
    w_cq_b, w_ck_b, w_cv_b, w_co_b = bf(w_cq), bf(w_ck), bf(w_cv), bf(w_co)
    w_gate_b, w_up_b, w_down_b = bf(w_gate), bf(w_up), bf(w_down)
    g_fin = g_final[None, :]
    heads = lambda t: t.reshape(1, t.shape[0], SB_HEADS, SB_HEAD_DIM, t.shape[2]).transpose(0, 1, 4, 2, 3)

    def tail(x2, attn, bg, u, init, mk, mv, seq_len, rows_per_batch):
        h = _mixer_out(x2, attn, bg, u, init, w_conv[l], vec(g_attn_out), vec(g_conv_out),
                       w_out_b, seq_len)
        h = _cross(h, vec(g_ca), w_cq_b, mk, mv, w_co_b, rows_per_batch)
        return _ffn(h, vec(g_ffn), w_gate_b, w_up_b, w_down_b, g_fin)

    xp = x_prompt.reshape(batch * seq, D_MODEL)
    xs = x_sample.reshape(n_s, D_MODEL)
    q, kt, vt, ktb, vtb, bg, u = _in_proj(xp, vec(g_mix), w_rows, w_kv_t, batch, seq)
    qs, kts, vts, _, _, bgs, us = _in_proj(xs, vec(g_mix), w_rows, w_kv_t, 1, n_s)
    ks, vs = kts[0].T, vts[0].T
    attn, attn_s = _sb_attention(q, ktb, vtb, sb_bias[l], batch, seq, qs.astype(F32), ks, vs,
                                 cache_sb_k[l], cache_sb_v[l], page_table)

    mk, mv = _memory_kv(mem_prompt.reshape(batch * N_MEM, D_MODEL), vec(g_mem), w_ck_b, w_cv_b)
    init_p = jnp.zeros((batch, SUBLANES, CONV_CH), F32)
    y_prompt = _tail(xp, attn, bg, u, init_p, w_conv[l], vec(g_attn_out), vec(g_conv_out), w_out_b,
                     vec(g_ca), w_cq_b, mk, mv, w_co_b,
                     vec(g_ffn), w_gate_b, w_up_b, w_down_b, g_fin, seq).reshape(batch, seq, D_MODEL)
    conv_p = u.reshape(batch, seq, CONV_CH)[:, seq - (CONV_K - 1):][None]
    mem_shape = (1, batch, N_MEM, CA_HEADS, CA_HEAD_DIM)

    init_s = jnp.pad(state_conv[l], ((0, 0), (SUBLANES - (CONV_K - 1), 0), (0, 0)))
    y_sample = tail(xs, attn_s, bgs, us, init_s,
                    cache_mem_k[l].reshape(dec_batch * N_MEM * CA_HEADS, CA_HEAD_DIM),
                    cache_mem_v[l].reshape(dec_batch * N_MEM * CA_HEADS, CA_HEAD_DIM),
                    dec_seq, dec_seq).reshape(dec_batch, dec_seq, D_MODEL)
    conv_s = us.reshape(dec_batch, dec_seq, CONV_CH)[:, dec_seq - (CONV_K - 1):][None]

    new_kv = lambda t: t.reshape(1, dec_batch, dec_seq, SB_HEADS, SB_HEAD_DIM)
    return (y_prompt, y_sample, heads(kt), heads(vt), conv_p,
            mk.reshape(mem_shape), mv.reshape(mem_shape), new_kv(ks), new_kv(vs), conv_s)
```
